```python
import jax, jax.numpy as jnp
from jax import lax
import numpy as np

D_MODEL = 1024
BATCH = 32
SEQ = 256
DEPTH = 4
DEC_BATCH = 8
DEC_SEQ = 1024
PAST_LEN = 512

GRID_W = 64
HEAD_DIM = 64
N_HEADS = 8
N_KV_HEADS = 2
GQA_GROUPS = N_HEADS // N_KV_HEADS
ATTN_W = N_HEADS * HEAD_DIM
KV_W = N_KV_HEADS * HEAD_DIM
WINDOW = 128
BLOCK = 128
SPAN = BLOCK + 2 * WINDOW
ROPE_BASE = 10000.0
AXIS_PAIRS = HEAD_DIM // 4
CONV_W = D_MODEL // 2
CONV_K = 31
CONV_PAD = CONV_K // 2
N_EXPERTS = 16
N_GROUPS = 4
EXPERTS_PER_GROUP = N_EXPERTS // N_GROUPS
TOPK_GROUPS = 1
TOP_K = 2
D_EXPERT = 512
IN_W = ATTN_W + 2 * KV_W + 2 * CONV_W + 2 * D_MODEL
SPLITS = (ATTN_W, ATTN_W + KV_W, ATTN_W + 2 * KV_W, ATTN_W + 2 * KV_W + 2 * CONV_W)
EPS = 1e-6
NEG = -1e30

kernel_name = "hybrid_dit_swa_conformer_moe_step"


def rms_norm(x, g):
    xf = x.astype(jnp.float32)
    y = xf * lax.rsqrt(jnp.mean(xf * xf, axis=-1, keepdims=True) + EPS)
    return (y * g.astype(jnp.float32)).astype(x.dtype)


def layer_norm(x, g, b):
    xf = x.astype(jnp.float32)
    mu = jnp.mean(xf, axis=-1, keepdims=True)
    var = jnp.mean(jnp.square(xf - mu), axis=-1, keepdims=True)
    y = (xf - mu) * lax.rsqrt(var + EPS) * g.astype(jnp.float32) + b.astype(jnp.float32)
    return y.astype(x.dtype)


def modulate(x, g, shift, scale):
    return rms_norm(x, g) * (1 + scale) + shift


def axial_angles(L):
    rows = L // GRID_W
    row = jnp.repeat(jnp.arange(rows), GRID_W).astype(jnp.float32)
    col = jnp.tile(jnp.arange(GRID_W), rows).astype(jnp.float32)
    inv = ROPE_BASE ** (-jnp.arange(AXIS_PAIRS, dtype=jnp.float32) / AXIS_PAIRS)
    return row[:, None] * inv, col[:, None] * inv


def rope_half(x, ang):
    cos = jnp.cos(ang)[:, None, :]
    sin = jnp.sin(ang)[:, None, :]
    cos = jnp.concatenate([cos, cos], axis=-1)
    sin = jnp.concatenate([sin, sin], axis=-1)
    xf = x.astype(jnp.float32)
    x1, x2 = jnp.split(xf, 2, axis=-1)
    return xf * cos + jnp.concatenate([-x2, x1], axis=-1) * sin


def axial_rope(x, ang_r, ang_c):
    xr, xc = jnp.split(x, 2, axis=-1)
    return jnp.concatenate([rope_half(xr, ang_r), rope_half(xc, ang_c)], axis=-1).astype(x.dtype)


def sink_softmax(logits, sink):
    s = jnp.broadcast_to(sink.astype(jnp.float32).reshape(N_KV_HEADS, GQA_GROUPS)[None, :, :, None, None],
                         logits.shape[:-1] + (1,))
    return jax.nn.softmax(jnp.concatenate([logits, s], axis=-1), axis=-1)[..., :-1]


def context_attention(q, k, v, sink):
    B, S = q.shape[:2]
    nb = S // BLOCK
    scale = HEAD_DIM ** -0.5
    qb = q.reshape(B, nb, BLOCK, N_KV_HEADS, GQA_GROUPS, HEAD_DIM).swapaxes(0, 1)

    def one_block(qi):
        s = jnp.einsum('bqhgd,bkhd->bhgqk', qi, k).astype(jnp.float32) * scale
        p = sink_softmax(s, sink).astype(v.dtype)
        return jnp.einsum('bhgqk,bkhd->bqhgd', p, v)

    o = lax.map(one_block, qb)
    return o.swapaxes(0, 1).reshape(B, S, ATTN_W)


def latent_attention(q, k, v, kc, vc, sink):
    B, L = q.shape[:2]
    nb = L // BLOCK
    scale = HEAD_DIM ** -0.5
    pad = ((0, 0), (WINDOW, WINDOW), (0, 0), (0, 0))
    kp = jnp.pad(k, pad)
    vp = jnp.pad(v, pad)
    qb = q.reshape(B, nb, BLOCK, N_KV_HEADS, GQA_GROUPS, HEAD_DIM).swapaxes(0, 1)

    def one_block(args):
        i, qi = args
        start = i * BLOCK
        kb = lax.dynamic_slice_in_dim(kp, start, SPAN, axis=1)
        vb = lax.dynamic_slice_in_dim(vp, start, SPAN, axis=1)
        qpos = start + jnp.arange(BLOCK)
        kpos = start - WINDOW + jnp.arange(SPAN)
        mask = (jnp.abs(qpos[:, None] - kpos[None, :]) <= WINDOW) & (kpos >= 0)[None, :] & (kpos < L)[None, :]
        s_loc = jnp.einsum('bqhgd,bkhd->bhgqk', qi, kb).astype(jnp.float32) * scale
        s_loc = jnp.where(mask, s_loc, NEG)
        s_ctx = jnp.einsum('bqhgd,bchd->bhgqc', qi, kc).astype(jnp.float32) * scale
        p = sink_softmax(jnp.concatenate([s_loc, s_ctx], axis=-1), sink).astype(v.dtype)
        return (jnp.einsum('bhgqk,bkhd->bqhgd', p[..., :SPAN], vb)
                + jnp.einsum('bhgqc,bchd->bqhgd', p[..., SPAN:], vc))

    o = lax.map(one_block, (jnp.arange(nb), qb))
    return o.swapaxes(0, 1).reshape(B, L, ATTN_W)


def conv_module(u, dw_w, dw_b, ln_g, ln_b):
    a, b = jnp.split(u, 2, axis=-1)
    y = a * jax.nn.sigmoid(b)
    y = lax.conv_general_dilated(y, dw_w[:, None, :].astype(y.dtype), window_strides=(1,),
                                 padding=[(CONV_PAD, CONV_PAD)],
                                 dimension_numbers=('NWC', 'WIO', 'NWC'),
                                 feature_group_count=CONV_W) + dw_b
    return jax.nn.silu(layer_norm(y, ln_g, ln_b))


def mixer(h, p, attend):
    B, L, _ = h.shape
    q, k, v, u, gates = jnp.split(h @ p['w_in'], SPLITS, axis=-1)
    q = rms_norm(q.reshape(B, L, N_HEADS, HEAD_DIM), p['q_norm'])
    k = rms_norm(k.reshape(B, L, N_KV_HEADS, HEAD_DIM), p['k_norm'])
    v = v.reshape(B, L, N_KV_HEADS, HEAD_DIM)
    attn = attend(q, k, v) @ p['w_attn_proj']
    conv = conv_module(u, p['dw_w'], p['dw_b'], p['cln_g'], p['cln_b']) @ p['w_conv_proj']
    g_attn, g_conv = jnp.split(jax.nn.sigmoid(gates), 2, axis=-1)
    return (g_attn * attn + g_conv * conv) @ p['w_out'], k, v


def moe(h, w_router, router_bias, w_gate, w_up, w_down):
    B, L, D = h.shape
    t = h.reshape(B * L, D)
    scores = jax.nn.sigmoid((t @ w_router).astype(jnp.float32))
    sel = scores + router_bias.astype(jnp.float32)
    grp_score = lax.top_k(sel.reshape(-1, N_GROUPS, EXPERTS_PER_GROUP), 2)[0].sum(-1)
    _, gidx = lax.top_k(grp_score, TOPK_GROUPS)
    gmask = jax.nn.one_hot(gidx, N_GROUPS, dtype=jnp.float32).sum(-2) > 0
    emask = jnp.repeat(gmask, EXPERTS_PER_GROUP, axis=-1)
    _, eidx = lax.top_k(jnp.where(emask, sel, NEG), TOP_K)
    w = jnp.take_along_axis(scores, eidx, axis=-1)
    w = w / jnp.sum(w, axis=-1, keepdims=True)
    combine = jnp.einsum('tk,tke->te', w, jax.nn.one_hot(eidx, N_EXPERTS, dtype=jnp.float32))
    hid = jax.nn.silu(jnp.einsum('td,edf->tef', t, w_gate)) * jnp.einsum('td,edf->tef', t, w_up)
    hid = hid * combine.astype(hid.dtype)[..., None]
    return jnp.einsum('tef,efd->td', hid, w_down).reshape(B, L, D)


def trunk_layer(x, cond, attend, p, w_router, router_bias):
    mod = jax.nn.silu(cond) @ p['w_ada'] + p['b_ada']
    sh1, sc1, g1, sh2, sc2, g2 = jnp.split(mod, 6, axis=-1)
    mix, k, v = mixer(modulate(x, p['norm1'], sh1, sc1), p, attend)
    x = x + g1 * mix
    h = modulate(x, p['norm2'], sh2, sc2)
    x = x + g2 * moe(h, w_router, router_bias, p['w_e_gate'], p['w_e_up'], p['w_e_down'])
    return x, k, v


def setup_inputs(seed: int = 0) -> dict:
    key = jax.random.key(seed)
    ks = iter(jax.random.split(key, 40))
    f32 = jnp.float32
    nrm = lambda shape, s=1.0: jax.random.normal(next(ks), shape, f32) * s
    gain = lambda shape: 1.0 + 0.02 * jax.random.normal(next(ks), shape, f32)
    D = D_MODEL
    return {
        'x_prompt': nrm((BATCH, SEQ, D)),
        'x_sample': nrm((DEC_BATCH, DEC_SEQ, D)),
        'cache_k': nrm((DEC_BATCH, DEPTH, PAST_LEN, N_KV_HEADS, HEAD_DIM)),
        'cache_v': nrm((DEC_BATCH, DEPTH, PAST_LEN, N_KV_HEADS, HEAD_DIM)),
        'c': nrm((DEC_BATCH, D)),
        'c_ctx': nrm((D,)),
        'w_ada': nrm((DEPTH, D, 6 * D), 0.5 * D ** -0.5),
        'b_ada': nrm((DEPTH, 6 * D), 0.02),
        'norm1': gain((DEPTH, D)),
        'norm2': gain((DEPTH, D)),
        'w_in': nrm((DEPTH, D, IN_W), D ** -0.5),
        'q_norm': gain((DEPTH, HEAD_DIM)),
        'k_norm': gain((DEPTH, HEAD_DIM)),
        'sink': nrm((DEPTH, N_HEADS), 0.5),
        'w_attn_proj': nrm((DEPTH, ATTN_W, D), ATTN_W ** -0.5),
        'dw_w': nrm((DEPTH, CONV_K, CONV_W), CONV_K ** -0.5),
        'dw_b': nrm((DEPTH, CONV_W), 0.02),
        'cln_g': gain((DEPTH, CONV_W)),
        'cln_b': nrm((DEPTH, CONV_W), 0.02),
        'w_conv_proj': nrm((DEPTH, CONV_W, D), CONV_W ** -0.5),
        'w_out': nrm((DEPTH, D, D), D ** -0.5),
        'w_router': nrm((D, N_EXPERTS), D ** -0.5),
        'router_bias': nrm((N_EXPERTS,), 0.01),
        'w_e_gate': nrm((DEPTH, N_EXPERTS, D, D_EXPERT), D ** -0.5),
        'w_e_up': nrm((DEPTH, N_EXPERTS, D, D_EXPERT), D ** -0.5),
        'w_e_down': nrm((DEPTH, N_EXPERTS, D_EXPERT, D), D_EXPERT ** -0.5),
        'final_norm': gain((D,)),
    }


def reference(x_prompt, x_sample, cache_k, cache_v, c, c_ctx, w_ada, b_ada, norm1, norm2,
              w_in, q_norm, k_norm, sink, w_attn_proj, dw_w, dw_b, cln_g, cln_b, w_conv_proj,
              w_out, w_router, router_bias, w_e_gate, w_e_up, w_e_down, final_norm):
    ang_r, ang_c = axial_angles(x_sample.shape[1])
    cond_ctx = c_ctx[None, None, :]
    cond_lat = c[:, None, :]
    xp, xs = x_prompt, x_sample
    new_ks, new_vs = [], []
    for l in range(DEPTH):
        p = {
            'w_ada': w_ada[l], 'b_ada': b_ada[l], 'norm1': norm1[l], 'norm2': norm2[l],
            'w_in': w_in[l], 'q_norm': q_norm[l], 'k_norm': k_norm[l],
            'w_attn_proj': w_attn_proj[l], 'dw_w': dw_w[l], 'dw_b': dw_b[l],
            'cln_g': cln_g[l], 'cln_b': cln_b[l], 'w_conv_proj': w_conv_proj[l],
            'w_out': w_out[l], 'w_e_gate': w_e_gate[l], 'w_e_up': w_e_up[l],
            'w_e_down': w_e_down[l],
        }
        sink_l = sink[l]
        kc_l = cache_k[:, l]
        vc_l = cache_v[:, l]

        ctx_attend = lambda q, k, v: context_attention(q, k, v, sink_l)
        xp, k_ctx, v_ctx = trunk_layer(xp, cond_ctx, ctx_attend, p, w_router, router_bias)
        new_ks.append(k_ctx)
        new_vs.append(v_ctx)

        lat_attend = lambda q, k, v: latent_attention(axial_rope(q, ang_r, ang_c),
                                                      axial_rope(k, ang_r, ang_c),
                                                      v, kc_l, vc_l, sink_l)
        xs, _, _ = trunk_layer(xs, cond_lat, lat_attend, p, w_router, router_bias)

    y_prompt = rms_norm(xp, final_norm)
    y_sample = rms_norm(xs, final_norm)
    new_k = jnp.stack(new_ks, axis=1)
    new_v = jnp.stack(new_vs, axis=1)
    return (y_prompt, y_sample, new_k, new_v)
```

```python
import jax
import jax.numpy as jnp
from jax import lax
from jax.experimental import pallas as pl
from jax.experimental.pallas import tpu as pltpu

F32 = jnp.float32
BF16 = jnp.bfloat16

D = 1024
DEPTH = 4
CTX_B, CTX_L = 32, 256
LAT_B, LAT_L = 8, 1024
PAST = 512
T_CTX = CTX_B * CTX_L
T_LAT = LAT_B * LAT_L
T_ALL = T_CTX + T_LAT
GRID_W = 64
HD = 64
N_HEADS = 8
N_KV = 2
ATTN_W = N_HEADS * HD
KV_W = N_KV * HD
WINDOW = 128
QBLK = 128
CONV_W = 512
CONV_K = 31
CONV_PAD = CONV_K // 2
N_EXP = 16
N_GROUPS = 4
EXP_PER_GROUP = N_EXP // N_GROUPS
D_EXP = 512
IN_W = ATTN_W + 2 * KV_W + 2 * CONV_W + 2 * D
OFF_K = ATTN_W
OFF_V = ATTN_W + KV_W
OFF_A = ATTN_W + 2 * KV_W
OFF_B = OFF_A + CONV_W
OFF_G = OFF_A + 2 * CONV_W
EPS = 1e-6
NEG = -1e30
ROPE_BASE = 10000.0
MOD_ROWS = 16

TM_IN = 512
TM_MERGE = 256
TM_CONV = 256
CONV_HALO = 16
TM_DISPATCH = 512
TM_FFN = 256
TM_COMBINE = 256
N_PAIRS = 2 * T_ALL
N_FFN_TILES = N_PAIRS // TM_FFN
N_UNITS = N_FFN_TILES + N_EXP - 1

VMEM_LIMIT = 56 * 1024 * 1024


def _sigmoid(x):
    return 1.0 / (1.0 + jnp.exp(-x))


def _mod_row(tile, tm):
    start = tile * tm
    return jnp.where(start < T_CTX, 0, 1 + (start - T_CTX) // LAT_L)


def _ada_body(cond_ref, w_ref, b_ref, o_ref):
    c = cond_ref[...]
    s = (c * _sigmoid(c)).astype(BF16)
    o_ref[0] = jnp.dot(s, w_ref[0].astype(BF16), preferred_element_type=F32) + b_ref[0]


def _ada_table(cond, w_ada, b_ada):
    nj = 6 * D // 1024
    return pl.pallas_call(
        _ada_body,
        out_shape=jax.ShapeDtypeStruct((DEPTH, MOD_ROWS, 6 * D), F32),
        grid=(DEPTH, nj),
        in_specs=[
            pl.BlockSpec((MOD_ROWS, D), lambda l, j: (0, 0)),
            pl.BlockSpec((1, D, 1024), lambda l, j: (l, 0, j)),
            pl.BlockSpec((1, 1, 1024), lambda l, j: (l, 0, j)),
        ],
        out_specs=pl.BlockSpec((1, MOD_ROWS, 1024), lambda l, j: (l, 0, j)),
        name="ada_table",
    )(cond, w_ada, b_ada.reshape(DEPTH, 1, 6 * D))


def _head_norm_rope(y, gain, cos, sin, bd, rm):
    ss = jnp.dot((y * y).astype(BF16), bd, preferred_element_type=F32)
    yn = y * lax.rsqrt(ss * (1.0 / HD) + EPS) * gain
    rot = jnp.dot(yn.astype(BF16), rm, preferred_element_type=F32)
    return yn * cos + rot * sin


def _inproj_body(x_ref, mod_ref, n1_ref, w_ref, qg_ref, kg_ref, cos_ref, sin_ref, bd_ref, rm_ref,
                 q_ref, k_ref, v_ref, u_ref, g_ref):
    row = _mod_row(pl.program_id(0), TM_IN)
    sh = mod_ref[pl.ds(row, 1), 0:D]
    sc = mod_ref[pl.ds(row, 1), D:2 * D]
    x = x_ref[...]
    ms = jnp.mean(x * x, axis=-1, keepdims=True)
    h = ((x * lax.rsqrt(ms + EPS) * n1_ref[...]) * (1.0 + sc) + sh).astype(BF16)

    def proj(lo, hi):
        return jnp.dot(h, w_ref[:, lo:hi], preferred_element_type=F32)

    cos = cos_ref[...]
    sin = sin_ref[...]
    q = _head_norm_rope(proj(0, OFF_K), qg_ref[...], cos, sin, bd_ref[...], rm_ref[...])
    q_ref[...] = (q * (HD ** -0.5)).astype(BF16)
    k = _head_norm_rope(proj(OFF_K, OFF_V), kg_ref[...], cos[:, :KV_W], sin[:, :KV_W],
                        bd_ref[:KV_W, :KV_W], rm_ref[:KV_W, :KV_W])
    k_ref[...] = k
    v_ref[...] = proj(OFF_V, OFF_A)
    a = proj(OFF_A, OFF_B)
    b = proj(OFF_B, OFF_G)
    u_ref[...] = (a * _sigmoid(b)).astype(BF16)
    for j in range(2):
        g = proj(OFF_G + j * D, OFF_G + (j + 1) * D)
        g_ref[:, j * D:(j + 1) * D] = _sigmoid(g).astype(BF16)


def _inproj(x, mod_l, n1, w_in, qg, kg, cos_t, sin_t, bd, rm):
    n_ctx_tiles = T_CTX // TM_IN
    tiles_per_seq = LAT_L // TM_IN

    def tab(i):
        return (jnp.where(i < n_ctx_tiles, 0, 1 + (i - n_ctx_tiles) % tiles_per_seq), 0)

    full = lambda shape: pl.BlockSpec(shape, lambda i: (0,) * len(shape))
    row = lambda w: pl.BlockSpec((TM_IN, w), lambda i: (i, 0))
    return pl.pallas_call(
        _inproj_body,
        out_shape=(
            jax.ShapeDtypeStruct((T_ALL, ATTN_W), BF16),
            jax.ShapeDtypeStruct((T_ALL, KV_W), F32),
            jax.ShapeDtypeStruct((T_ALL, KV_W), F32),
            jax.ShapeDtypeStruct((T_ALL, CONV_W), BF16),
            jax.ShapeDtypeStruct((T_ALL, 2 * D), BF16),
        ),
        grid=(T_ALL // TM_IN,),
        in_specs=[
            row(D), full((MOD_ROWS, 6 * D)), full((1, D)), full((D, IN_W)),
            full((1, ATTN_W)), full((1, KV_W)),
            pl.BlockSpec((TM_IN, ATTN_W), tab), pl.BlockSpec((TM_IN, ATTN_W), tab),
            full((ATTN_W, ATTN_W)), full((ATTN_W, ATTN_W)),
        ],
        out_specs=(row(ATTN_W), row(KV_W), row(KV_W), row(CONV_W), row(2 * D)),
        compiler_params=pltpu.CompilerParams(dimension_semantics=("arbitrary",), vmem_limit_bytes=VMEM_LIMIT),
        name="inproj",
    )(x, mod_l, n1, w_in, qg, kg, cos_t, sin_t, bd, rm)


def _head_pair_kv(k, v):
    lane = lax.broadcasted_iota(jnp.int32, k.shape, 1)
    low = lane < HD
    zero = jnp.zeros_like(k)
    k_sw = pltpu.roll(k, HD, 1)
    v_sw = pltpu.roll(v, HD, 1)
    g0 = (jnp.where(low, k, zero), jnp.where(low, v, zero), jnp.where(low, zero, k_sw), jnp.where(low, zero, v_sw))
    g1 = (jnp.where(low, k_sw, zero), jnp.where(low, v_sw, zero), jnp.where(low, zero, k), jnp.where(low, zero, v))
    return tuple(tuple(t.astype(BF16) for t in g) for g in (g0, g1))


def _sink_attend(qp, kk, vv, sink, mask):
    s = lax.dot_general(qp, kk, (((1,), (1,)), ((), ())), preferred_element_type=F32)
    if mask is not None:
        s = jnp.where(mask, s, NEG)
    m = jnp.maximum(jnp.max(s, axis=-1, keepdims=True), sink)
    p = jnp.exp(s - m)
    den = jnp.sum(p, axis=-1, keepdims=True) + jnp.exp(sink - m)
    return jnp.dot(p.astype(BF16), vv, preferred_element_type=F32) / den


def _attend_all_heads(sink_ref, q_ref, o_ref, kv, mask):
    for pair in range(N_HEADS // 2):
        k_lo, v_lo, k_hi, v_hi = kv[pair // 2]
        qp = q_ref[:, pair * 128:(pair + 1) * 128]
        o = (_sink_attend(qp, k_lo, v_lo, sink_ref[2 * pair], mask)
             + _sink_attend(qp, k_hi, v_hi, sink_ref[2 * pair + 1], mask))
        o_ref[:, pair * 128:(pair + 1) * 128] = o.astype(BF16)


def _attn_ctx_body(sink_ref, q_ref, k_ref, v_ref, o_ref):
    _attend_all_heads(sink_ref, q_ref, o_ref, _head_pair_kv(k_ref[...], v_ref[...]), None)


def _attn_ctx(sink_l, q, k, v):
    blk = lambda w: pl.BlockSpec((CTX_L, w), lambda b: (b, 0))
    return pl.pallas_call(
        _attn_ctx_body,
        out_shape=jax.ShapeDtypeStruct((T_CTX, ATTN_W), BF16),
        grid=(CTX_B,),
        in_specs=[pl.BlockSpec(memory_space=pltpu.SMEM), blk(ATTN_W), blk(KV_W), blk(KV_W)],
        out_specs=blk(ATTN_W),
        compiler_params=pltpu.CompilerParams(dimension_semantics=("arbitrary",)),
        name="attn_ctx",
    )(sink_l, q, k, v)


def _attn_lat_body(sink_ref, q_ref, kp_ref, kc_ref, kn_ref, vp_ref, vc_ref, vn_ref, ck_ref, cv_ref, o_ref):
    i = pl.program_id(1)
    nblk = LAT_L // QBLK
    k = jnp.concatenate([kp_ref[...], kc_ref[...], kn_ref[...], ck_ref[0]], axis=0)
    v = jnp.concatenate([vp_ref[...], vc_ref[...], vn_ref[...], cv_ref[0]], axis=0)
    nk = 3 * QBLK + PAST
    r = lax.broadcasted_iota(jnp.int32, (QBLK, nk), 0)
    c = lax.broadcasted_iota(jnp.int32, (QBLK, nk), 1)
    local = (c - r >= 0) & (c - r <= 2 * WINDOW)
    local = local & ((c >= QBLK) | (i > 0)) & ((c < 2 * QBLK) | (i < nblk - 1))
    mask = local | (c >= 3 * QBLK)
    _attend_all_heads(sink_ref, q_ref, o_ref, _head_pair_kv(k, v), mask)


def _attn_lat(sink_l, q, k, v, cache_k_l, cache_v_l):
    nblk = LAT_L // QBLK
    base = T_CTX // QBLK
    cur = lambda b, i: (base + b * nblk + i, 0)
    prev = lambda b, i: (base + b * nblk + jnp.maximum(i - 1, 0), 0)
    nxt = lambda b, i: (base + b * nblk + jnp.minimum(i + 1, nblk - 1), 0)
    kvb = lambda f: pl.BlockSpec((QBLK, KV_W), f)
    cache = pl.BlockSpec((1, PAST, KV_W), lambda b, i: (b, 0, 0))
    return pl.pallas_call(
        _attn_lat_body,
        out_shape=jax.ShapeDtypeStruct((T_LAT, ATTN_W), BF16),
        grid=(LAT_B, nblk),
        in_specs=[pl.BlockSpec(memory_space=pltpu.SMEM), pl.BlockSpec((QBLK, ATTN_W), cur),
                  kvb(prev), kvb(cur), kvb(nxt), kvb(prev), kvb(cur), kvb(nxt), cache, cache],
        out_specs=pl.BlockSpec((QBLK, ATTN_W), lambda b, i: (b * nblk + i, 0)),
        compiler_params=pltpu.CompilerParams(dimension_semantics=("arbitrary", "arbitrary")),
        name="attn_lat",
    )(sink_l, q, k, k, k, v, v, v, cache_k_l, cache_v_l)


def _conv_body(up_ref, uc_ref, un_ref, w_ref, b_ref, g_ref, beta_ref, o_ref, pad_ref):
    i = pl.program_id(0)
    n_ctx = T_CTX // TM_CONV
    per_seq = LAT_L // TM_CONV
    j = (i - n_ctx) % per_seq
    has_prev = (i >= n_ctx) & (j > 0)
    has_next = (i >= n_ctx) & (j < per_seq - 1)
    zero = jnp.zeros((CONV_HALO, CONV_W), F32)
    pad_ref[0:CONV_HALO, :] = jnp.where(has_prev, up_ref[...].astype(F32), zero)
    pad_ref[CONV_HALO:CONV_HALO + TM_CONV, :] = uc_ref[...].astype(F32)
    pad_ref[CONV_HALO + TM_CONV:, :] = jnp.where(has_next, un_ref[...].astype(F32), zero)
    chunk = 64
    for c0 in range(0, TM_CONV, chunk):
        acc = jnp.zeros((chunk, CONV_W), F32) + b_ref[...]
        for t in range(CONV_K):
            src = CONV_HALO - CONV_PAD + c0 + t
            acc = acc + pad_ref[src:src + chunk, :] * w_ref[t:t + 1, :]
        mu = jnp.mean(acc, axis=-1, keepdims=True)
        cen = acc - mu
        var = jnp.mean(cen * cen, axis=-1, keepdims=True)
        y = cen * lax.rsqrt(var + EPS) * g_ref[...] + beta_ref[...]
        o_ref[c0:c0 + chunk, :] = (y * _sigmoid(y)).astype(BF16)


def _conv(u, dw_w, dw_b, ln_g, ln_b):
    per = TM_CONV // CONV_HALO
    last = T_ALL // CONV_HALO - 1
    full = lambda shape: pl.BlockSpec(shape, lambda i: (0,) * len(shape))
    return pl.pallas_call(
        _conv_body,
        out_shape=jax.ShapeDtypeStruct((T_ALL, CONV_W), BF16),
        grid=(T_ALL // TM_CONV,),
        in_specs=[
            pl.BlockSpec((CONV_HALO, CONV_W), lambda i: (jnp.maximum(i * per - 1, 0), 0)),
            pl.BlockSpec((TM_CONV, CONV_W), lambda i: (i, 0)),
            pl.BlockSpec((CONV_HALO, CONV_W), lambda i: (jnp.minimum((i + 1) * per, last), 0)),
            full((CONV_K, CONV_W)), full((1, CONV_W)), full((1, CONV_W)), full((1, CONV_W)),
        ],
        out_specs=pl.BlockSpec((TM_CONV, CONV_W), lambda i: (i, 0)),
        scratch_shapes=[pltpu.VMEM((TM_CONV + 2 * CONV_HALO, CONV_W), F32)],
        compiler_params=pltpu.CompilerParams(dimension_semantics=("arbitrary",)),
        name="conv_module",
    )(u, u, u, dw_w, dw_b, ln_g, ln_b)


def _row_pick(idx, rows):
    out = jnp.zeros_like(rows[0])
    for e, r in enumerate(rows):
        out = jnp.where(idx == e, r, out)
    return out


def _argmax_rows(rows):
    best = rows[0]
    idx = jnp.zeros(rows[0].shape, jnp.int32)
    for e in range(1, len(rows)):
        upd = rows[e] > best
        idx = jnp.where(upd, e, idx)
        best = jnp.where(upd, rows[e], best)
    return idx


def _route(logits_t, bias):
    scores = _sigmoid(logits_t)
    sel = scores + bias
    s_rows = [scores[e:e + 1, :] for e in range(N_EXP)]
    rows = [sel[e:e + 1, :] for e in range(N_EXP)]
    grp = []
    for g in range(N_GROUPS):
        r = rows[g * EXP_PER_GROUP:(g + 1) * EXP_PER_GROUP]
        best = None
        for a in range(EXP_PER_GROUP):
            for b in range(a + 1, EXP_PER_GROUP):
                s = r[a] + r[b]
                best = s if best is None else jnp.maximum(best, s)
        grp.append(best)
    gidx = _argmax_rows(grp)
    masked = [jnp.where(gidx == e // EXP_PER_GROUP, rows[e], NEG) for e in range(N_EXP)]
    i1 = _argmax_rows(masked)
    i2 = _argmax_rows([jnp.where(i1 == e, -jnp.inf, masked[e]) for e in range(N_EXP)])
    w1 = _row_pick(i1, s_rows)
    w2 = _row_pick(i2, s_rows)
    tot = w1 + w2
    return i1, i2, w1 / tot, w2 / tot


def _merge_body(a_ref, c_ref, g_ref, x_ref, mod_ref, n2_ref, wap_ref, wcp_ref, wo_ref, wr_ref, rb_ref,
                x1_ref, h_ref, ri_ref, rw_ref, cnt_ref, cnt_scr):
    step = pl.program_id(0)
    tm = TM_MERGE
    row = _mod_row(step, tm)
    g1 = mod_ref[pl.ds(row, 1), 2 * D:3 * D]
    sh2 = mod_ref[pl.ds(row, 1), 3 * D:4 * D]
    sc2 = mod_ref[pl.ds(row, 1), 4 * D:5 * D]
    attn = jnp.dot(a_ref[...], wap_ref[...], preferred_element_type=F32)
    conv = jnp.dot(c_ref[...], wcp_ref[...], preferred_element_type=F32)
    merged = g_ref[:, 0:D].astype(F32) * attn + g_ref[:, D:2 * D].astype(F32) * conv
    mix = jnp.dot(merged.astype(BF16), wo_ref[...], preferred_element_type=F32)
    x1 = x_ref[...] + g1 * mix
    x1_ref[...] = x1
    ms = jnp.mean(x1 * x1, axis=-1, keepdims=True)
    h = (x1 * lax.rsqrt(ms + EPS) * n2_ref[...]) * (1.0 + sc2) + sh2
    h_ref[...] = h

    logits_t = lax.dot_general(wr_ref[...], h, (((1,), (1,)), ((), ())),
                               precision=lax.Precision.HIGHEST, preferred_element_type=F32)
    i1, i2, w1, w2 = _route(logits_t, rb_ref[...])

    @pl.when(step == 0)
    def _():
        cnt_scr[...] = jnp.zeros_like(cnt_scr)

    eio = lax.broadcasted_iota(jnp.int32, (N_EXP, tm), 0)
    oh1 = eio == i1
    oh2 = eio == i2
    oh = (oh1 | oh2).astype(F32)
    tri = (lax.broadcasted_iota(jnp.int32, (tm, tm), 0) < lax.broadcasted_iota(jnp.int32, (tm, tm), 1))
    prefix = jnp.dot(oh.astype(BF16), tri.astype(BF16), preferred_element_type=F32)
    pos = cnt_scr[:, 0:1] + prefix
    r1 = jnp.sum(jnp.where(oh1, pos, 0.0), axis=0, keepdims=True).astype(jnp.int32)
    r2 = jnp.sum(jnp.where(oh2, pos, 0.0), axis=0, keepdims=True).astype(jnp.int32)
    cnt_scr[...] = cnt_scr[...] + jnp.sum(oh, axis=1, keepdims=True)
    cnt_ref[...] = cnt_scr[...]

    sub = lax.broadcasted_iota(jnp.int32, (8, tm), 0)
    zi = jnp.zeros((8, tm), jnp.int32)
    ri_ref[...] = jnp.where(sub == 0, i1, jnp.where(sub == 1, i2, jnp.where(sub == 2, r1, jnp.where(sub == 3, r2, zi))))
    rw_ref[...] = jnp.where(sub == 0, w1, jnp.where(sub == 1, w2, jnp.zeros((8, tm), F32)))


def _merge(attn_o, conv_o, gates, x, mod_l, n2, wap, wcp, wo, wr_t, rb):
    tm = TM_MERGE
    full = lambda shape: pl.BlockSpec(shape, lambda i: (0,) * len(shape))
    row = lambda w: pl.BlockSpec((tm, w), lambda i: (i, 0))
    lanes = pl.BlockSpec((8, tm), lambda i: (0, i))
    return pl.pallas_call(
        _merge_body,
        out_shape=(
            jax.ShapeDtypeStruct((T_ALL, D), F32),
            jax.ShapeDtypeStruct((T_ALL, D), F32),
            jax.ShapeDtypeStruct((8, T_ALL), jnp.int32),
            jax.ShapeDtypeStruct((8, T_ALL), F32),
            jax.ShapeDtypeStruct((N_EXP, 128), F32),
        ),
        grid=(T_ALL // tm,),
        in_specs=[row(ATTN_W), row(CONV_W), row(2 * D), row(D), full((MOD_ROWS, 6 * D)), full((1, D)),
                  full((ATTN_W, D)), full((CONV_W, D)), full((D, D)), full((N_EXP, D)), full((N_EXP, 1))],
        out_specs=(row(D), row(D), lanes, lanes, full((N_EXP, 128))),
        scratch_shapes=[pltpu.VMEM((N_EXP, 128), F32)],
        compiler_params=pltpu.CompilerParams(dimension_semantics=("arbitrary",), vmem_limit_bytes=VMEM_LIMIT),
        name="merge_router",
    )(attn_o, conv_o, gates, x, mod_l, n2, wap, wcp, wo, wr_t, rb)


def _dispatch_body(dest_ref, h_hbm, xs_hbm, sem):
    base = pl.program_id(0) * TM_DISPATCH

    def copy(t, k):
        return pltpu.make_async_copy(h_hbm.at[pl.ds(base + t, 1), :],
                                     xs_hbm.at[pl.ds(dest_ref[k, t], 1), :], sem)

    def issue(t, carry):
        copy(t, 0).start()
        copy(t, 1).start()
        return carry

    lax.fori_loop(0, TM_DISPATCH, issue, 0, unroll=8)

    def drain(t, carry):
        copy(t, 0).wait()
        copy(t, 1).wait()
        return carry

    lax.fori_loop(0, TM_DISPATCH, drain, 0, unroll=8)


def _dispatch(dest, h):
    return pl.pallas_call(
        _dispatch_body,
        out_shape=jax.ShapeDtypeStruct((N_PAIRS, D), F32),
        grid=(T_ALL // TM_DISPATCH,),
        in_specs=[pl.BlockSpec((2, TM_DISPATCH), lambda i: (0, i), memory_space=pltpu.SMEM),
                  pl.BlockSpec(memory_space=pl.ANY)],
        out_specs=pl.BlockSpec(memory_space=pl.ANY),
        scratch_shapes=[pltpu.SemaphoreType.DMA],
        compiler_params=pltpu.CompilerParams(dimension_semantics=("arbitrary",)),
        name="dispatch",
    )(dest, h)


def _ffn_body(tile_ref, exp_ref, lo_ref, hi_ref, first_ref, valid_ref, x_ref, wg_ref, wu_ref, wd_ref, o_ref):
    u = pl.program_id(0)

    @pl.when(valid_ref[u] == 1)
    def _():
        x = x_ref[...].astype(BF16)
        g = jnp.dot(x, wg_ref[0], preferred_element_type=F32)
        up = jnp.dot(x, wu_ref[0], preferred_element_type=F32)
        rows = lax.broadcasted_iota(jnp.int32, (TM_FFN, 1), 0)
        mine = (rows >= lo_ref[u]) & (rows < hi_ref[u])
        hid = jnp.where(mine, g * _sigmoid(g) * up, 0.0).astype(BF16)
        y = jnp.dot(hid, wd_ref[0], preferred_element_type=F32)

        @pl.when(first_ref[u] == 1)
        def _():
            o_ref[...] = y

        @pl.when(first_ref[u] == 0)
        def _():
            o_ref[...] += y


def _ffn(units, xs, wg, wu, wd):
    xmap = lambda u, tile, exp, lo, hi, first, valid: (tile[u], 0)
    wmap = lambda u, tile, exp, lo, hi, first, valid: (exp[u], 0, 0)
    return pl.pallas_call(
        _ffn_body,
        out_shape=jax.ShapeDtypeStruct((N_PAIRS, D), F32),
        grid_spec=pltpu.PrefetchScalarGridSpec(
            num_scalar_prefetch=6,
            grid=(N_UNITS,),
            in_specs=[pl.BlockSpec((TM_FFN, D), xmap),
                      pl.BlockSpec((1, D, D_EXP), wmap), pl.BlockSpec((1, D, D_EXP), wmap),
                      pl.BlockSpec((1, D_EXP, D), wmap)],
            out_specs=pl.BlockSpec((TM_FFN, D), xmap),
        ),
        compiler_params=pltpu.CompilerParams(dimension_semantics=("arbitrary",), vmem_limit_bytes=VMEM_LIMIT),
        name="expert_ffn",
    )(*units, xs, wg, wu, wd)


def _ffn_units(counts):
    ends = jnp.cumsum(counts)
    starts = ends - counts
    t0 = jnp.arange(N_FFN_TILES, dtype=jnp.int32) * TM_FFN
    e_first = jnp.sum(ends[None, :] <= t0[:, None], axis=1).astype(jnp.int32)
    e_last = jnp.sum(ends[None, :] <= (t0 + TM_FFN - 1)[:, None], axis=1).astype(jnp.int32)
    n_per = e_last - e_first + 1
    u_end = jnp.cumsum(n_per)
    u_start = u_end - n_per
    total = u_end[-1]
    u = jnp.arange(N_UNITS, dtype=jnp.int32)
    uc = jnp.minimum(u, total - 1)
    tile = jnp.sum(u_end[None, :] <= uc[:, None], axis=1).astype(jnp.int32)
    exp = e_first[tile] + (uc - u_start[tile])
    lo = jnp.clip(starts[exp] - tile * TM_FFN, 0, TM_FFN)
    hi = jnp.clip(ends[exp] - tile * TM_FFN, 0, TM_FFN)
    valid = (u < total).astype(jnp.int32)
    first = ((uc == u_start[tile]) & (u < total)).astype(jnp.int32)
    return tile, exp.astype(jnp.int32), lo.astype(jnp.int32), hi.astype(jnp.int32), first, valid


def _combine_body(dest_ref, x1_ref, w_ref, mod_ref, ys_hbm, o_ref, buf, sem):
    tm = TM_COMBINE

    def copy(t, k):
        return pltpu.make_async_copy(ys_hbm.at[pl.ds(dest_ref[k, t], 1), :], buf.at[k, pl.ds(t, 1), :], sem)

    def issue(t, carry):
        copy(t, 0).start()
        copy(t, 1).start()
        return carry

    lax.fori_loop(0, tm, issue, 0, unroll=8)
    row = _mod_row(pl.program_id(0), tm)
    g2 = mod_ref[pl.ds(row, 1), 5 * D:6 * D]

    def drain(t, carry):
        copy(t, 0).wait()
        copy(t, 1).wait()
        return carry

    lax.fori_loop(0, tm, drain, 0, unroll=8)
    y = w_ref[:, 0:1] * buf[0] + w_ref[:, 1:2] * buf[1]
    o_ref[...] = x1_ref[...] + g2 * y


def _combine(dest, x1, w_tok, mod_l, ys):
    tm = TM_COMBINE
    return pl.pallas_call(
        _combine_body,
        out_shape=jax.ShapeDtypeStruct((T_ALL, D), F32),
        grid=(T_ALL // tm,),
        in_specs=[pl.BlockSpec((2, tm), lambda i: (0, i), memory_space=pltpu.SMEM),
                  pl.BlockSpec((tm, D), lambda i: (i, 0)),
                  pl.BlockSpec((tm, 2), lambda i: (i, 0)),
                  pl.BlockSpec((MOD_ROWS, 6 * D), lambda i: (0, 0)),
                  pl.BlockSpec(memory_space=pl.ANY)],
        out_specs=pl.BlockSpec((tm, D), lambda i: (i, 0)),
        scratch_shapes=[pltpu.VMEM((2, tm, D), F32), pltpu.SemaphoreType.DMA],
        compiler_params=pltpu.CompilerParams(dimension_semantics=("arbitrary",)),
        name="combine",
    )(dest, x1, w_tok, mod_l, ys)


def _final_norm_body(x_ref, g_ref, o_ref):
    x = x_ref[...]
    ms = jnp.mean(x * x, axis=-1, keepdims=True)
    o_ref[...] = x * lax.rsqrt(ms + EPS) * g_ref[...]


def _final_norm(x, gain):
    tm = 512
    return pl.pallas_call(
        _final_norm_body,
        out_shape=jax.ShapeDtypeStruct((T_ALL, D), F32),
        grid=(T_ALL // tm,),
        in_specs=[pl.BlockSpec((tm, D), lambda i: (i, 0)), pl.BlockSpec((1, D), lambda i: (0, 0))],
        out_specs=pl.BlockSpec((tm, D), lambda i: (i, 0)),
        name="final_norm",
    )(x, gain)


def _rope_tables():
    pos = jnp.arange(LAT_L)
    rowp = (pos // GRID_W).astype(F32)
    colp = (pos % GRID_W).astype(F32)
    pairs = HD // 4
    inv = ROPE_BASE ** (-jnp.arange(pairs, dtype=F32) / pairs)
    ang = jnp.concatenate([rowp[:, None] * inv] * 2 + [colp[:, None] * inv] * 2, axis=-1)
    cos = jnp.tile(jnp.cos(ang), (1, N_HEADS))
    sin = jnp.tile(jnp.sin(ang), (1, N_HEADS))
    cos = jnp.concatenate([jnp.ones((TM_IN, ATTN_W), F32), cos], axis=0)
    sin = jnp.concatenate([jnp.zeros((TM_IN, ATTN_W), F32), sin], axis=0)
    return cos, sin


def _head_matrices():
    i = jnp.arange(ATTN_W)
    bd = (i[:, None] // HD == i[None, :] // HD).astype(BF16)
    half = HD // 4
    j = i[None, :]
    src = i[:, None]
    first = (j % (2 * half)) < half
    rm = jnp.where(first & (src == j + half), -1.0, 0.0) + jnp.where(~first & (src == j - half), 1.0, 0.0)
    return bd, rm.astype(BF16)


def kernel(x_prompt, x_sample, cache_k, cache_v, c, c_ctx, w_ada, b_ada, norm1, norm2, w_in, q_norm, k_norm,
           sink, w_attn_proj, dw_w, dw_b, cln_g, cln_b, w_conv_proj, w_out, w_router, router_bias,
           w_e_gate, w_e_up, w_e_down, final_norm):
    x = jnp.concatenate([x_prompt.reshape(T_CTX, D), x_sample.reshape(T_LAT, D)], axis=0)
    cond = jnp.concatenate([c_ctx[None, :], c, jnp.zeros((MOD_ROWS - 1 - LAT_B, D), F32)], axis=0)
    mod = _ada_table(cond, w_ada, b_ada)
    cos_t, sin_t = _rope_tables()
    bd, rm = _head_matrices()
    w_in_b = w_in.astype(BF16)
    wap_b = w_attn_proj.astype(BF16)
    wcp_b = w_conv_proj.astype(BF16)
    wo_b = w_out.astype(BF16)
    wg_b = w_e_gate.astype(BF16)
    wu_b = w_e_up.astype(BF16)
    wd_b = w_e_down.astype(BF16)
    wr_t = w_router.T
    rb = router_bias.reshape(N_EXP, 1)
    ck = cache_k.reshape(LAT_B, DEPTH, PAST, KV_W)
    cv = cache_v.reshape(LAT_B, DEPTH, PAST, KV_W)

    new_k, new_v = [], []
    for l in range(DEPTH):
        q, k, v, u, gates = _inproj(x, mod[l], norm1[l][None, :], w_in_b[l],
                                    jnp.tile(q_norm[l], N_HEADS)[None, :], jnp.tile(k_norm[l], N_KV)[None, :],
                                    cos_t, sin_t, bd, rm)
        new_k.append(k[:T_CTX].reshape(CTX_B, CTX_L, N_KV, HD))
        new_v.append(v[:T_CTX].reshape(CTX_B, CTX_L, N_KV, HD))
        o_ctx = _attn_ctx(sink[l], q, k, v)
        o_lat = _attn_lat(sink[l], q, k, v, ck[:, l], cv[:, l])
        attn_o = jnp.concatenate([o_ctx, o_lat], axis=0)
        conv_o = _conv(u, dw_w[l], dw_b[l][None, :], cln_g[l][None, :], cln_b[l][None, :])
        x1, h, ri, rw, cnt = _merge(attn_o, conv_o, gates, x, mod[l], norm2[l][None, :],
                                    wap_b[l], wcp_b[l], wo_b[l], wr_t, rb)
        counts = cnt[:, 0].astype(jnp.int32)
        offs = jnp.cumsum(counts) - counts
        dest = offs[ri[0:2]] + ri[2:4]
        xs = _dispatch(dest, h)
        ys = _ffn(_ffn_units(counts), xs, wg_b[l], wu_b[l], wd_b[l])
        x = _combine(dest, x1, rw[0:2].T, mod[l], ys)

    y = _final_norm(x, final_norm[None, :])
    y_prompt = y[:T_CTX].reshape(CTX_B, CTX_L, D)
    y_sample = y[T_CTX:].reshape(LAT_B, LAT_L, D)
    return y_prompt, y_sample, jnp.stack(new_k, axis=1), jnp.stack(new_v, axis=1)
```

```python
import jax
import jax.numpy as jnp
from jax import lax
from jax.experimental import pallas as pl
from jax.experimental.pallas import tpu as pltpu

F32 = jnp.float32
BF16 = jnp.bfloat16

D = 1024
DEPTH = 4
CTX_B, CTX_L = 32, 256
LAT_B, LAT_L = 8, 1024
PAST = 512
T_CTX = CTX_B * CTX_L
T_LAT = LAT_B * LAT_L
T_ALL = T_CTX + T_LAT
GRID_W = 64
HD = 64
N_HEADS = 8
N_KV = 2
ATTN_W = N_HEADS * HD
KV_W = N_KV * HD
WINDOW = 128
QBLK = 128
CONV_W = 512
CONV_K = 31
CONV_PAD = CONV_K // 2
N_EXP = 16
N_GROUPS = 4
EXP_PER_GROUP = N_EXP // N_GROUPS
D_EXP = 512
IN_W = ATTN_W + 2 * KV_W + 2 * CONV_W + 2 * D
OFF_K = ATTN_W
OFF_V = ATTN_W + KV_W
OFF_A = ATTN_W + 2 * KV_W
OFF_B = OFF_A + CONV_W
OFF_G = OFF_A + 2 * CONV_W
EPS = 1e-6
NEG = -1e30
ROPE_BASE = 10000.0
MOD_ROWS = 16

TM_IN = 512
TM_MERGE = 512
TM_ROUTE = 2048
ROUTE_CHUNK = 256
TM_CONV = 256
CONV_HALO = 16
TM_DISPATCH = 512
TM_FFN = 256
TM_COMBINE = 256
N_PAIRS = 2 * T_ALL
N_FFN_TILES = N_PAIRS // TM_FFN
N_UNITS = N_FFN_TILES + N_EXP - 1

VMEM_LIMIT = 56 * 1024 * 1024


def _sigmoid(x):
    return 1.0 / (1.0 + jnp.exp(-x))


def _mod_row(tile, tm):
    start = tile * tm
    return jnp.where(start < T_CTX, 0, 1 + (start - T_CTX) // LAT_L)


def _ada_body(cond_ref, w_ref, b_ref, o_ref):
    c = cond_ref[...]
    s = (c * _sigmoid(c)).astype(BF16)
    o_ref[0] = jnp.dot(s, w_ref[0].astype(BF16), preferred_element_type=F32) + b_ref[0]


def _ada_table(cond, w_ada, b_ada):
    nj = 6 * D // 1024
    return pl.pallas_call(
        _ada_body,
        out_shape=jax.ShapeDtypeStruct((DEPTH, MOD_ROWS, 6 * D), F32),
        grid=(DEPTH, nj),
        in_specs=[
            pl.BlockSpec((MOD_ROWS, D), lambda l, j: (0, 0)),
            pl.BlockSpec((1, D, 1024), lambda l, j: (l, 0, j)),
            pl.BlockSpec((1, 1, 1024), lambda l, j: (l, 0, j)),
        ],
        out_specs=pl.BlockSpec((1, MOD_ROWS, 1024), lambda l, j: (l, 0, j)),
        name="ada_table",
    )(cond, w_ada, b_ada.reshape(DEPTH, 1, 6 * D))


def _head_norm_rope(y, gain, cos, sin, bd, rm):
    ss = jnp.dot((y * y).astype(BF16), bd, preferred_element_type=F32)
    yn = y * lax.rsqrt(ss * (1.0 / HD) + EPS) * gain
    rot = jnp.dot(yn.astype(BF16), rm, preferred_element_type=F32)
    return yn * cos + rot * sin


def _inproj_body(x_ref, mod_ref, n1_ref, w_ref, qg_ref, kg_ref, cos_ref, sin_ref, bd_ref, rm_ref,
                 q_ref, k_ref, v_ref, u_ref, g_ref):
    row = _mod_row(pl.program_id(0), TM_IN)
    sh = mod_ref[pl.ds(row, 1), 0:D]
    sc = mod_ref[pl.ds(row, 1), D:2 * D]
    x = x_ref[...]
    ms = jnp.mean(x * x, axis=-1, keepdims=True)
    h = ((x * lax.rsqrt(ms + EPS) * n1_ref[...]) * (1.0 + sc) + sh).astype(BF16)

    def proj(lo, hi):
        return jnp.dot(h, w_ref[:, lo:hi], preferred_element_type=F32)

    cos = cos_ref[...]
    sin = sin_ref[...]
    q = _head_norm_rope(proj(0, OFF_K), qg_ref[...], cos, sin, bd_ref[...], rm_ref[...])
    q_ref[...] = (q * (HD ** -0.5)).astype(BF16)
    k = _head_norm_rope(proj(OFF_K, OFF_V), kg_ref[...], cos[:, :KV_W], sin[:, :KV_W],
                        bd_ref[:KV_W, :KV_W], rm_ref[:KV_W, :KV_W])
    k_ref[...] = k
    v_ref[...] = proj(OFF_V, OFF_A)
    a = proj(OFF_A, OFF_B)
    b = proj(OFF_B, OFF_G)
    u_ref[...] = (a * _sigmoid(b)).astype(BF16)
    for j in range(2):
        g = proj(OFF_G + j * D, OFF_G + (j + 1) * D)
        g_ref[:, j * D:(j + 1) * D] = _sigmoid(g).astype(BF16)


def _inproj(x, mod_l, n1, w_in, qg, kg, cos_t, sin_t, bd, rm):
    n_ctx_tiles = T_CTX // TM_IN
    tiles_per_seq = LAT_L // TM_IN

    def tab(i):
        return (jnp.where(i < n_ctx_tiles, 0, 1 + (i - n_ctx_tiles) % tiles_per_seq), 0)

    full = lambda shape: pl.BlockSpec(shape, lambda i: (0,) * len(shape))
    row = lambda w: pl.BlockSpec((TM_IN, w), lambda i: (i, 0))
    return pl.pallas_call(
        _inproj_body,
        out_shape=(
            jax.ShapeDtypeStruct((T_ALL, ATTN_W), BF16),
            jax.ShapeDtypeStruct((T_ALL, KV_W), F32),
            jax.ShapeDtypeStruct((T_ALL, KV_W), F32),
            jax.ShapeDtypeStruct((T_ALL, CONV_W), BF16),
            jax.ShapeDtypeStruct((T_ALL, 2 * D), BF16),
        ),
        grid=(T_ALL // TM_IN,),
        in_specs=[
            row(D), full((MOD_ROWS, 6 * D)), full((1, D)), full((D, IN_W)),
            full((1, ATTN_W)), full((1, KV_W)),
            pl.BlockSpec((TM_IN, ATTN_W), tab), pl.BlockSpec((TM_IN, ATTN_W), tab),
            full((ATTN_W, ATTN_W)), full((ATTN_W, ATTN_W)),
        ],
        out_specs=(row(ATTN_W), row(KV_W), row(KV_W), row(CONV_W), row(2 * D)),
        compiler_params=pltpu.CompilerParams(dimension_semantics=("arbitrary",), vmem_limit_bytes=VMEM_LIMIT),
        name="inproj",
    )(x, mod_l, n1, w_in, qg, kg, cos_t, sin_t, bd, rm)


def _head_pair_kv(k, v):
    lane = lax.broadcasted_iota(jnp.int32, k.shape, 1)
    low = lane < HD
    zero = jnp.zeros_like(k)
    k_sw = pltpu.roll(k, HD, 1)
    v_sw = pltpu.roll(v, HD, 1)
    g0 = (jnp.where(low, k, zero), jnp.where(low, v, zero), jnp.where(low, zero, k_sw), jnp.where(low, zero, v_sw))
    g1 = (jnp.where(low, k_sw, zero), jnp.where(low, v_sw, zero), jnp.where(low, zero, k), jnp.where(low, zero, v))
    return tuple(tuple(t.astype(BF16) for t in g) for g in (g0, g1))


def _sink_attend(qp, kk, vv, sink, mask):
    s = lax.dot_general(qp, kk, (((1,), (1,)), ((), ())), preferred_element_type=F32)
    if mask is not None:
        s = jnp.where(mask, s, NEG)
    m = jnp.maximum(jnp.max(s, axis=-1, keepdims=True), sink)
    p = jnp.exp(s - m)
    den = jnp.sum(p, axis=-1, keepdims=True) + jnp.exp(sink - m)
    return jnp.dot(p.astype(BF16), vv, preferred_element_type=F32) / den


def _attend_all_heads(sink_ref, q_ref, o_ref, kv, mask):
    for pair in range(N_HEADS // 2):
        k_lo, v_lo, k_hi, v_hi = kv[pair // 2]
        qp = q_ref[:, pair * 128:(pair + 1) * 128]
        o = (_sink_attend(qp, k_lo, v_lo, sink_ref[2 * pair], mask)
             + _sink_attend(qp, k_hi, v_hi, sink_ref[2 * pair + 1], mask))
        o_ref[:, pair * 128:(pair + 1) * 128] = o.astype(BF16)


def _attn_ctx_body(sink_ref, q_ref, k_ref, v_ref, o_ref):
    _attend_all_heads(sink_ref, q_ref, o_ref, _head_pair_kv(k_ref[...], v_ref[...]), None)


def _attn_ctx(sink_l, q, k, v):
    blk = lambda w: pl.BlockSpec((CTX_L, w), lambda b: (b, 0))
    return pl.pallas_call(
        _attn_ctx_body,
        out_shape=jax.ShapeDtypeStruct((T_CTX, ATTN_W), BF16),
        grid=(CTX_B,),
        in_specs=[pl.BlockSpec(memory_space=pltpu.SMEM), blk(ATTN_W), blk(KV_W), blk(KV_W)],
        out_specs=blk(ATTN_W),
        compiler_params=pltpu.CompilerParams(dimension_semantics=("arbitrary",)),
        name="attn_ctx",
    )(sink_l, q, k, v)


def _attn_lat_body(sink_ref, q_ref, kp_ref, kc_ref, kn_ref, vp_ref, vc_ref, vn_ref, ck_ref, cv_ref, o_ref):
    i = pl.program_id(1)
    nblk = LAT_L // QBLK
    k = jnp.concatenate([kp_ref[...], kc_ref[...], kn_ref[...], ck_ref[0]], axis=0)
    v = jnp.concatenate([vp_ref[...], vc_ref[...], vn_ref[...], cv_ref[0]], axis=0)
    nk = 3 * QBLK + PAST
    r = lax.broadcasted_iota(jnp.int32, (QBLK, nk), 0)
    c = lax.broadcasted_iota(jnp.int32, (QBLK, nk), 1)
    local = (c - r >= 0) & (c - r <= 2 * WINDOW)
    local = local & ((c >= QBLK) | (i > 0)) & ((c < 2 * QBLK) | (i < nblk - 1))
    mask = local | (c >= 3 * QBLK)
    _attend_all_heads(sink_ref, q_ref, o_ref, _head_pair_kv(k, v), mask)


def _attn_lat(sink_l, q, k, v, cache_k_l, cache_v_l):
    nblk = LAT_L // QBLK
    base = T_CTX // QBLK
    cur = lambda b, i: (base + b * nblk + i, 0)
    prev = lambda b, i: (base + b * nblk + jnp.maximum(i - 1, 0), 0)
    nxt = lambda b, i: (base + b * nblk + jnp.minimum(i + 1, nblk - 1), 0)
    kvb = lambda f: pl.BlockSpec((QBLK, KV_W), f)
    cache = pl.BlockSpec((1, PAST, KV_W), lambda b, i: (b, 0, 0))
    return pl.pallas_call(
        _attn_lat_body,
        out_shape=jax.ShapeDtypeStruct((T_LAT, ATTN_W), BF16),
        grid=(LAT_B, nblk),
        in_specs=[pl.BlockSpec(memory_space=pltpu.SMEM), pl.BlockSpec((QBLK, ATTN_W), cur),
                  kvb(prev), kvb(cur), kvb(nxt), kvb(prev), kvb(cur), kvb(nxt), cache, cache],
        out_specs=pl.BlockSpec((QBLK, ATTN_W), lambda b, i: (b * nblk + i, 0)),
        compiler_params=pltpu.CompilerParams(dimension_semantics=("arbitrary", "arbitrary")),
        name="attn_lat",
    )(sink_l, q, k, k, k, v, v, v, cache_k_l, cache_v_l)


def _conv_body(up_ref, uc_ref, un_ref, w_ref, b_ref, g_ref, beta_ref, o_ref, pad_ref):
    i = pl.program_id(0)
    n_ctx = T_CTX // TM_CONV
    per_seq = LAT_L // TM_CONV
    j = (i - n_ctx) % per_seq
    has_prev = (i >= n_ctx) & (j > 0)
    has_next = (i >= n_ctx) & (j < per_seq - 1)
    zero = jnp.zeros((CONV_HALO, CONV_W), F32)
    pad_ref[0, 0:CONV_HALO, :] = jnp.where(has_prev, up_ref[...].astype(F32), zero)
    pad_ref[0, CONV_HALO:CONV_HALO + TM_CONV, :] = uc_ref[...].astype(F32)
    pad_ref[0, CONV_HALO + TM_CONV:, :] = jnp.where(has_next, un_ref[...].astype(F32), zero)
    n_sh = TM_CONV + 2 * CONV_HALO - 8
    for s in range(1, 8):
        pad_ref[s, 0:n_sh, :] = pad_ref[0, s:s + n_sh, :]
    chunk = 64
    for c0 in range(0, TM_CONV, chunk):
        acc = jnp.zeros((chunk, CONV_W), F32) + b_ref[...]
        for t in range(CONV_K):
            src = CONV_HALO - CONV_PAD + t
            row = src - src % 8 + c0
            acc = acc + pad_ref[src % 8, row:row + chunk, :] * w_ref[t:t + 1, :]
        mu = jnp.mean(acc, axis=-1, keepdims=True)
        cen = acc - mu
        var = jnp.mean(cen * cen, axis=-1, keepdims=True)
        y = cen * lax.rsqrt(var + EPS) * g_ref[...] + beta_ref[...]
        o_ref[c0:c0 + chunk, :] = (y * _sigmoid(y)).astype(BF16)


def _conv(u, dw_w, dw_b, ln_g, ln_b):
    per = TM_CONV // CONV_HALO
    last = T_ALL // CONV_HALO - 1
    full = lambda shape: pl.BlockSpec(shape, lambda i: (0,) * len(shape))
    return pl.pallas_call(
        _conv_body,
        out_shape=jax.ShapeDtypeStruct((T_ALL, CONV_W), BF16),
        grid=(T_ALL // TM_CONV,),
        in_specs=[
            pl.BlockSpec((CONV_HALO, CONV_W), lambda i: (jnp.maximum(i * per - 1, 0), 0)),
            pl.BlockSpec((TM_CONV, CONV_W), lambda i: (i, 0)),
            pl.BlockSpec((CONV_HALO, CONV_W), lambda i: (jnp.minimum((i + 1) * per, last), 0)),
            full((CONV_K, CONV_W)), full((1, CONV_W)), full((1, CONV_W)), full((1, CONV_W)),
        ],
        out_specs=pl.BlockSpec((TM_CONV, CONV_W), lambda i: (i, 0)),
        scratch_shapes=[pltpu.VMEM((8, TM_CONV + 2 * CONV_HALO, CONV_W), F32)],
        compiler_params=pltpu.CompilerParams(dimension_semantics=("arbitrary",)),
        name="conv_module",
    )(u, u, u, dw_w, dw_b, ln_g, ln_b)


def _row_pick(idx, rows):
    out = jnp.zeros_like(rows[0])
    for e, r in enumerate(rows):
        out = jnp.where(idx == e, r, out)
    return out


def _argmax_rows(rows):
    best = rows[0]
    idx = jnp.zeros(rows[0].shape, jnp.int32)
    for e in range(1, len(rows)):
        upd = rows[e] > best
        idx = jnp.where(upd, e, idx)
        best = jnp.where(upd, rows[e], best)
    return idx


def _route(logits_t, bias):
    scores = _sigmoid(logits_t)
    sel = scores + bias
    s_rows = [scores[e:e + 1, :] for e in range(N_EXP)]
    rows = [sel[e:e + 1, :] for e in range(N_EXP)]
    grp = []
    for g in range(N_GROUPS):
        r = rows[g * EXP_PER_GROUP:(g + 1) * EXP_PER_GROUP]
        best = None
        for a in range(EXP_PER_GROUP):
            for b in range(a + 1, EXP_PER_GROUP):
                s = r[a] + r[b]
                best = s if best is None else jnp.maximum(best, s)
        grp.append(best)
    gidx = _argmax_rows(grp)
    masked = [jnp.where(gidx == e // EXP_PER_GROUP, rows[e], NEG) for e in range(N_EXP)]
    i1 = _argmax_rows(masked)
    i2 = _argmax_rows([jnp.where(i1 == e, -jnp.inf, masked[e]) for e in range(N_EXP)])
    w1 = _row_pick(i1, s_rows)
    w2 = _row_pick(i2, s_rows)
    tot = w1 + w2
    return i1, i2, w1 / tot, w2 / tot


def _merge_body(a_ref, c_ref, g_ref, x_ref, mod_ref, n2_ref, wap_ref, wcp_ref, wo_ref, wr_ref,
                x1_ref, h_ref, lg_ref):
    row = _mod_row(pl.program_id(0), TM_MERGE)
    g1 = mod_ref[pl.ds(row, 1), 2 * D:3 * D]
    sh2 = mod_ref[pl.ds(row, 1), 3 * D:4 * D]
    sc2 = mod_ref[pl.ds(row, 1), 4 * D:5 * D]
    attn = jnp.dot(a_ref[...], wap_ref[...], preferred_element_type=F32)
    conv = jnp.dot(c_ref[...], wcp_ref[...], preferred_element_type=F32)
    merged = g_ref[:, 0:D].astype(F32) * attn + g_ref[:, D:2 * D].astype(F32) * conv
    mix = jnp.dot(merged.astype(BF16), wo_ref[...], preferred_element_type=F32)
    x1 = x_ref[...] + g1 * mix
    x1_ref[...] = x1
    ms = jnp.mean(x1 * x1, axis=-1, keepdims=True)
    h = (x1 * lax.rsqrt(ms + EPS) * n2_ref[...]) * (1.0 + sc2) + sh2
    h_ref[...] = h
    lg_ref[...] = lax.dot_general(wr_ref[...], h, (((1,), (1,)), ((), ())),
                                  precision=lax.Precision.HIGHEST, preferred_element_type=F32)


def _merge(attn_o, conv_o, gates, x, mod_l, n2, wap, wcp, wo, wr_t):
    tm = TM_MERGE
    full = lambda shape: pl.BlockSpec(shape, lambda i: (0,) * len(shape))
    row = lambda w: pl.BlockSpec((tm, w), lambda i: (i, 0))
    return pl.pallas_call(
        _merge_body,
        out_shape=(
            jax.ShapeDtypeStruct((T_ALL, D), F32),
            jax.ShapeDtypeStruct((T_ALL, D), F32),
            jax.ShapeDtypeStruct((N_EXP, T_ALL), F32),
        ),
        grid=(T_ALL // tm,),
        in_specs=[row(ATTN_W), row(CONV_W), row(2 * D), row(D), full((MOD_ROWS, 6 * D)), full((1, D)),
                  full((ATTN_W, D)), full((CONV_W, D)), full((D, D)), full((N_EXP, D))],
        out_specs=(row(D), row(D), pl.BlockSpec((N_EXP, tm), lambda i: (0, i))),
        compiler_params=pltpu.CompilerParams(dimension_semantics=("arbitrary",), vmem_limit_bytes=VMEM_LIMIT),
        name="merge",
    )(attn_o, conv_o, gates, x, mod_l, n2, wap, wcp, wo, wr_t)


def _router_body(lg_ref, rb_ref, ri_ref, rw_ref, cnt_ref, cnt_scr):
    tm = TM_ROUTE
    i1, i2, w1, w2 = _route(lg_ref[...], rb_ref[...])

    @pl.when(pl.program_id(0) == 0)
    def _():
        cnt_scr[...] = jnp.zeros_like(cnt_scr)

    eio = lax.broadcasted_iota(jnp.int32, (N_EXP, tm), 0)
    oh1 = eio == i1
    oh2 = eio == i2
    oh = (oh1 | oh2).astype(F32)
    nc = ROUTE_CHUNK
    tri = (lax.broadcasted_iota(jnp.int32, (nc, nc), 0) < lax.broadcasted_iota(jnp.int32, (nc, nc), 1)).astype(BF16)
    base = cnt_scr[:, 0:1]
    pos = []
    for c0 in range(0, tm, nc):
        ohc = oh[:, c0:c0 + nc]
        pos.append(base + jnp.dot(ohc.astype(BF16), tri, preferred_element_type=F32))
        base = base + jnp.sum(ohc, axis=1, keepdims=True)
    pos = jnp.concatenate(pos, axis=1)
    r1 = jnp.sum(jnp.where(oh1, pos, 0.0), axis=0, keepdims=True).astype(jnp.int32)
    r2 = jnp.sum(jnp.where(oh2, pos, 0.0), axis=0, keepdims=True).astype(jnp.int32)
    cnt_scr[...] = jnp.broadcast_to(base, cnt_scr.shape)
    cnt_ref[...] = jnp.broadcast_to(base, cnt_ref.shape)

    sub = lax.broadcasted_iota(jnp.int32, (8, tm), 0)
    zi = jnp.zeros((8, tm), jnp.int32)
    ri_ref[...] = jnp.where(sub == 0, i1, jnp.where(sub == 1, i2, jnp.where(sub == 2, r1, jnp.where(sub == 3, r2, zi))))
    rw_ref[...] = jnp.where(sub == 0, w1, jnp.where(sub == 1, w2, jnp.zeros((8, tm), F32)))


def _router(logits_t, rb):
    tm = TM_ROUTE
    lanes = lambda rows: pl.BlockSpec((rows, tm), lambda i: (0, i))
    full = lambda shape: pl.BlockSpec(shape, lambda i: (0,) * len(shape))
    return pl.pallas_call(
        _router_body,
        out_shape=(
            jax.ShapeDtypeStruct((8, T_ALL), jnp.int32),
            jax.ShapeDtypeStruct((8, T_ALL), F32),
            jax.ShapeDtypeStruct((N_EXP, 128), F32),
        ),
        grid=(T_ALL // tm,),
        in_specs=[lanes(N_EXP), full((N_EXP, 1))],
        out_specs=(lanes(8), lanes(8), full((N_EXP, 128))),
        scratch_shapes=[pltpu.VMEM((N_EXP, 128), F32)],
        compiler_params=pltpu.CompilerParams(dimension_semantics=("arbitrary",)),
        name="router",
    )(logits_t, rb)


def _dispatch_body(dest_ref, h_ref, xs_hbm, sem):
    def copy(t, k):
        return pltpu.make_async_copy(h_ref.at[pl.ds(t, 1), :], xs_hbm.at[pl.ds(dest_ref[k, t], 1), :], sem)

    def issue(t, carry):
        copy(t, 0).start()
        copy(t, 1).start()
        return carry

    lax.fori_loop(0, TM_DISPATCH, issue, 0, unroll=8)

    def drain(t, carry):
        copy(t, 0).wait()
        copy(t, 1).wait()
        return carry

    lax.fori_loop(0, TM_DISPATCH, drain, 0, unroll=8)


def _dispatch(dest, h):
    return pl.pallas_call(
        _dispatch_body,
        out_shape=jax.ShapeDtypeStruct((N_PAIRS, D), F32),
        grid=(T_ALL // TM_DISPATCH,),
        in_specs=[pl.BlockSpec((2, TM_DISPATCH), lambda i: (0, i), memory_space=pltpu.SMEM),
                  pl.BlockSpec((TM_DISPATCH, D), lambda i: (i, 0))],
        out_specs=pl.BlockSpec(memory_space=pl.ANY),
        scratch_shapes=[pltpu.SemaphoreType.DMA],
        compiler_params=pltpu.CompilerParams(dimension_semantics=("arbitrary",)),
        name="dispatch",
    )(dest, h)


def _ffn_body(tile_ref, exp_ref, lo_ref, hi_ref, first_ref, valid_ref, x_ref, wg_ref, wu_ref, wd_ref, o_ref):
    u = pl.program_id(0)

    @pl.when(valid_ref[u] == 1)
    def _():
        x = x_ref[...].astype(BF16)
        g = jnp.dot(x, wg_ref[0], preferred_element_type=F32)
        up = jnp.dot(x, wu_ref[0], preferred_element_type=F32)
        rows = lax.broadcasted_iota(jnp.int32, (TM_FFN, 1), 0)
        mine = (rows >= lo_ref[u]) & (rows < hi_ref[u])
        hid = jnp.where(mine, g * _sigmoid(g) * up, 0.0).astype(BF16)
        y = jnp.dot(hid, wd_ref[0], preferred_element_type=F32)

        @pl.when(first_ref[u] == 1)
        def _():
            o_ref[...] = y

        @pl.when(first_ref[u] == 0)
        def _():
            o_ref[...] += y


def _ffn(units, xs, wg, wu, wd):
    xmap = lambda u, tile, exp, lo, hi, first, valid: (tile[u], 0)
    wmap = lambda u, tile, exp, lo, hi, first, valid: (exp[u], 0, 0)
    return pl.pallas_call(
        _ffn_body,
        out_shape=jax.ShapeDtypeStruct((N_PAIRS, D), F32),
        grid_spec=pltpu.PrefetchScalarGridSpec(
            num_scalar_prefetch=6,
            grid=(N_UNITS,),
            in_specs=[pl.BlockSpec((TM_FFN, D), xmap),
                      pl.BlockSpec((1, D, D_EXP), wmap), pl.BlockSpec((1, D, D_EXP), wmap),
                      pl.BlockSpec((1, D_EXP, D), wmap)],
            out_specs=pl.BlockSpec((TM_FFN, D), xmap),
        ),
        compiler_params=pltpu.CompilerParams(dimension_semantics=("arbitrary",), vmem_limit_bytes=VMEM_LIMIT),
        name="expert_ffn",
    )(*units, xs, wg, wu, wd)


def _ffn_units(counts):
    ends = jnp.cumsum(counts)
    starts = ends - counts
    t0 = jnp.arange(N_FFN_TILES, dtype=jnp.int32) * TM_FFN
    e_first = jnp.sum(ends[None, :] <= t0[:, None], axis=1).astype(jnp.int32)
    e_last = jnp.sum(ends[None, :] <= (t0 + TM_FFN - 1)[:, None], axis=1).astype(jnp.int32)
    n_per = e_last - e_first + 1
    u_end = jnp.cumsum(n_per)
    u_start = u_end - n_per
    total = u_end[-1]
    u = jnp.arange(N_UNITS, dtype=jnp.int32)
    uc = jnp.minimum(u, total - 1)
    tile = jnp.sum(u_end[None, :] <= uc[:, None], axis=1).astype(jnp.int32)
    exp = e_first[tile] + (uc - u_start[tile])
    lo = jnp.clip(starts[exp] - tile * TM_FFN, 0, TM_FFN)
    hi = jnp.clip(ends[exp] - tile * TM_FFN, 0, TM_FFN)
    valid = (u < total).astype(jnp.int32)
    first = ((uc == u_start[tile]) & (u < total)).astype(jnp.int32)
    return tile, exp.astype(jnp.int32), lo.astype(jnp.int32), hi.astype(jnp.int32), first, valid


def _combine_body(dest_ref, x1_ref, w_ref, mod_ref, ys_hbm, o_ref, buf, sem):
    tm = TM_COMBINE

    def copy(t, k):
        return pltpu.make_async_copy(ys_hbm.at[pl.ds(dest_ref[k, t], 1), :], buf.at[k, pl.ds(t, 1), :], sem)

    def issue(t, carry):
        copy(t, 0).start()
        copy(t, 1).start()
        return carry

    lax.fori_loop(0, tm, issue, 0, unroll=8)
    row = _mod_row(pl.program_id(0), tm)
    g2 = mod_ref[pl.ds(row, 1), 5 * D:6 * D]

    def drain(t, carry):
        copy(t, 0).wait()
        copy(t, 1).wait()
        return carry

    lax.fori_loop(0, tm, drain, 0, unroll=8)
    y = w_ref[:, 0:1] * buf[0] + w_ref[:, 1:2] * buf[1]
    o_ref[...] = x1_ref[...] + g2 * y


def _combine(dest, x1, w_tok, mod_l, ys):
    tm = TM_COMBINE
    return pl.pallas_call(
        _combine_body,
        out_shape=jax.ShapeDtypeStruct((T_ALL, D), F32),
        grid=(T_ALL // tm,),
        in_specs=[pl.BlockSpec((2, tm), lambda i: (0, i), memory_space=pltpu.SMEM),
                  pl.BlockSpec((tm, D), lambda i: (i, 0)),
                  pl.BlockSpec((tm, 2), lambda i: (i, 0)),
                  pl.BlockSpec((MOD_ROWS, 6 * D), lambda i: (0, 0)),
                  pl.BlockSpec(memory_space=pl.ANY)],
        out_specs=pl.BlockSpec((tm, D), lambda i: (i, 0)),
        scratch_shapes=[pltpu.VMEM((2, tm, D), F32), pltpu.SemaphoreType.DMA],
        compiler_params=pltpu.CompilerParams(dimension_semantics=("arbitrary",)),
        name="combine",
    )(dest, x1, w_tok, mod_l, ys)


def _final_norm_body(x_ref, g_ref, o_ref):
    x = x_ref[...]
    ms = jnp.mean(x * x, axis=-1, keepdims=True)
    o_ref[...] = x * lax.rsqrt(ms + EPS) * g_ref[...]


def _final_norm(x, gain):
    tm = 512
    return pl.pallas_call(
        _final_norm_body,
        out_shape=jax.ShapeDtypeStruct((T_ALL, D), F32),
        grid=(T_ALL // tm,),
        in_specs=[pl.BlockSpec((tm, D), lambda i: (i, 0)), pl.BlockSpec((1, D), lambda i: (0, 0))],
        out_specs=pl.BlockSpec((tm, D), lambda i: (i, 0)),
        name="final_norm",
    )(x, gain)


def _rope_tables():
    pos = jnp.arange(LAT_L)
    rowp = (pos // GRID_W).astype(F32)
    colp = (pos % GRID_W).astype(F32)
    pairs = HD // 4
    inv = ROPE_BASE ** (-jnp.arange(pairs, dtype=F32) / pairs)
    ang = jnp.concatenate([rowp[:, None] * inv] * 2 + [colp[:, None] * inv] * 2, axis=-1)
    cos = jnp.tile(jnp.cos(ang), (1, N_HEADS))
    sin = jnp.tile(jnp.sin(ang), (1, N_HEADS))
    cos = jnp.concatenate([jnp.ones((TM_IN, ATTN_W), F32), cos], axis=0)
    sin = jnp.concatenate([jnp.zeros((TM_IN, ATTN_W), F32), sin], axis=0)
    return cos, sin


def _head_matrices():
    i = jnp.arange(ATTN_W)
    bd = (i[:, None] // HD == i[None, :] // HD).astype(BF16)
    half = HD // 4
    j = i[None, :]
    src = i[:, None]
    first = (j % (2 * half)) < half
    rm = jnp.where(first & (src == j + half), -1.0, 0.0) + jnp.where(~first & (src == j - half), 1.0, 0.0)
    return bd, rm.astype(BF16)


def kernel(x_prompt, x_sample, cache_k, cache_v, c, c_ctx, w_ada, b_ada, norm1, norm2, w_in, q_norm, k_norm,
           sink, w_attn_proj, dw_w, dw_b, cln_g, cln_b, w_conv_proj, w_out, w_router, router_bias,
           w_e_gate, w_e_up, w_e_down, final_norm):
    x = jnp.concatenate([x_prompt.reshape(T_CTX, D), x_sample.reshape(T_LAT, D)], axis=0)
    cond = jnp.concatenate([c_ctx[None, :], c, jnp.zeros((MOD_ROWS - 1 - LAT_B, D), F32)], axis=0)
    mod = _ada_table(cond, w_ada, b_ada)
    cos_t, sin_t = _rope_tables()
    bd, rm = _head_matrices()
    w_in_b = w_in.astype(BF16)
    wap_b = w_attn_proj.astype(BF16)
    wcp_b = w_conv_proj.astype(BF16)
    wo_b = w_out.astype(BF16)
    wg_b = w_e_gate.astype(BF16)
    wu_b = w_e_up.astype(BF16)
    wd_b = w_e_down.astype(BF16)
    wr_t = w_router.T
    rb = router_bias.reshape(N_EXP, 1)
    ck = cache_k.reshape(LAT_B, DEPTH, PAST, KV_W)
    cv = cache_v.reshape(LAT_B, DEPTH, PAST, KV_W)

    new_k, new_v = [], []
    for l in range(DEPTH):
        q, k, v, u, gates = _inproj(x, mod[l], norm1[l][None, :], w_in_b[l],
                                    jnp.tile(q_norm[l], N_HEADS)[None, :], jnp.tile(k_norm[l], N_KV)[None, :],
                                    cos_t, sin_t, bd, rm)
        new_k.append(k[:T_CTX].reshape(CTX_B, CTX_L, N_KV, HD))
        new_v.append(v[:T_CTX].reshape(CTX_B, CTX_L, N_KV, HD))
        o_ctx = _attn_ctx(sink[l], q, k, v)
        o_lat = _attn_lat(sink[l], q, k, v, ck[:, l], cv[:, l])
        attn_o = jnp.concatenate([o_ctx, o_lat], axis=0)
        conv_o = _conv(u, dw_w[l], dw_b[l][None, :], cln_g[l][None, :], cln_b[l][None, :])
        x1, h, logits_t = _merge(attn_o, conv_o, gates, x, mod[l], norm2[l][None, :],
                                 wap_b[l], wcp_b[l], wo_b[l], wr_t)
        ri, rw, cnt = _router(logits_t, rb)
        counts = cnt[:, 0].astype(jnp.int32)
        offs = jnp.cumsum(counts) - counts
        seg = jnp.sum(jnp.where(ri[0:2, :, None] == jnp.arange(N_EXP, dtype=jnp.int32), offs, 0), axis=-1)
        dest = seg + ri[2:4]
        xs = _dispatch(dest, h)
        ys = _ffn(_ffn_units(counts), xs, wg_b[l], wu_b[l], wd_b[l])
        x = _combine(dest, x1, rw[0:2].T, mod[l], ys)

    y = _final_norm(x, final_norm[None, :])
    y_prompt = y[:T_CTX].reshape(CTX_B, CTX_L, D)
    y_sample = y[T_CTX:].reshape(LAT_B, LAT_L, D)
    return y_prompt, y_sample, jnp.stack(new_k, axis=1), jnp.stack(new_v, axis=1)
```

```python
import jax
import jax.numpy as jnp
from jax import lax
from jax.experimental import pallas as pl
from jax.experimental.pallas import tpu as pltpu

F32 = jnp.float32
BF16 = jnp.bfloat16

D = 1024
DEPTH = 4
CTX_B, CTX_L = 32, 256
LAT_B, LAT_L = 8, 1024
PAST = 512
T_CTX = CTX_B * CTX_L
T_LAT = LAT_B * LAT_L
T_ALL = T_CTX + T_LAT
GRID_W = 64
HD = 64
N_HEADS = 8
N_KV = 2
ATTN_W = N_HEADS * HD
KV_W = N_KV * HD
WINDOW = 128
QBLK = 128
CONV_W = 512
CONV_K = 31
CONV_PAD = CONV_K // 2
N_EXP = 16
N_GROUPS = 4
EXP_PER_GROUP = N_EXP // N_GROUPS
D_EXP = 512
IN_W = ATTN_W + 2 * KV_W + 2 * CONV_W + 2 * D
OFF_K = ATTN_W
OFF_V = ATTN_W + KV_W
OFF_A = ATTN_W + 2 * KV_W
OFF_B = OFF_A + CONV_W
OFF_G = OFF_A + 2 * CONV_W
EPS = 1e-6
NEG = -1e30
ROPE_BASE = 10000.0
MOD_ROWS = 16

TM_IN = 512
TM_MERGE = 512
TM_ROUTE = 2048
ROUTE_CHUNK = 256
TM_CONV = 256
CONV_HALO = 16
SORT_TL = 512
N_SORT = T_ALL // SORT_TL
CHUNK = 16
CAP = 2 * SORT_TL + 2 * 128
N_CHUNK = CAP // CHUNK
TM_FFN = 256
XS_ROWS = 2 * T_ALL + N_SORT * N_EXP * (CHUNK - 1)
N_FFN_TILES = XS_ROWS // TM_FFN
N_UNITS = N_FFN_TILES + N_EXP - 1

VMEM_LIMIT = 56 * 1024 * 1024


def _sigmoid(x):
    return 1.0 / (1.0 + jnp.exp(-x))


def _mod_row(tile, tm):
    start = tile * tm
    return jnp.where(start < T_CTX, 0, 1 + (start - T_CTX) // LAT_L)


def _ada_body(cond_ref, w_ref, b_ref, o_ref):
    c = cond_ref[...]
    s = (c * _sigmoid(c)).astype(BF16)
    o_ref[0] = jnp.dot(s, w_ref[0].astype(BF16), preferred_element_type=F32) + b_ref[0]


def _ada_table(cond, w_ada, b_ada):
    nj = 6 * D // 1024
    return pl.pallas_call(
        _ada_body,
        out_shape=jax.ShapeDtypeStruct((DEPTH, MOD_ROWS, 6 * D), F32),
        grid=(DEPTH, nj),
        in_specs=[
            pl.BlockSpec((MOD_ROWS, D), lambda l, j: (0, 0)),
            pl.BlockSpec((1, D, 1024), lambda l, j: (l, 0, j)),
            pl.BlockSpec((1, 1, 1024), lambda l, j: (l, 0, j)),
        ],
        out_specs=pl.BlockSpec((1, MOD_ROWS, 1024), lambda l, j: (l, 0, j)),
        name="ada_table",
    )(cond, w_ada, b_ada.reshape(DEPTH, 1, 6 * D))


def _head_norm_rope(y, gain, cos, sin, bd, rm):
    ss = jnp.dot((y * y).astype(BF16), bd, preferred_element_type=F32)
    yn = y * lax.rsqrt(ss * (1.0 / HD) + EPS) * gain
    rot = jnp.dot(yn.astype(BF16), rm, preferred_element_type=F32)
    return yn * cos + rot * sin


def _inproj_body(x_ref, mod_ref, n1_ref, w_ref, qg_ref, kg_ref, cos_ref, sin_ref, bd_ref, rm_ref,
                 q_ref, k_ref, v_ref, u_ref, g_ref):
    row = _mod_row(pl.program_id(0), TM_IN)
    sh = mod_ref[pl.ds(row, 1), 0:D]
    sc = mod_ref[pl.ds(row, 1), D:2 * D]
    x = x_ref[...]
    ms = jnp.mean(x * x, axis=-1, keepdims=True)
    h = ((x * lax.rsqrt(ms + EPS) * n1_ref[...]) * (1.0 + sc) + sh).astype(BF16)

    def proj(lo, hi):
        return jnp.dot(h, w_ref[:, lo:hi], preferred_element_type=F32)

    cos = cos_ref[...]
    sin = sin_ref[...]
    q = _head_norm_rope(proj(0, OFF_K), qg_ref[...], cos, sin, bd_ref[...], rm_ref[...])
    q_ref[...] = (q * (HD ** -0.5)).astype(BF16)
    k = _head_norm_rope(proj(OFF_K, OFF_V), kg_ref[...], cos[:, :KV_W], sin[:, :KV_W],
                        bd_ref[:KV_W, :KV_W], rm_ref[:KV_W, :KV_W])
    k_ref[...] = k
    v_ref[...] = proj(OFF_V, OFF_A)
    a = proj(OFF_A, OFF_B)
    b = proj(OFF_B, OFF_G)
    u_ref[...] = (a * _sigmoid(b)).astype(BF16)
    for j in range(2):
        g = proj(OFF_G + j * D, OFF_G + (j + 1) * D)
        g_ref[:, j * D:(j + 1) * D] = _sigmoid(g).astype(BF16)


def _inproj(x, mod_l, n1, w_in, qg, kg, cos_t, sin_t, bd, rm):
    n_ctx_tiles = T_CTX // TM_IN
    tiles_per_seq = LAT_L // TM_IN

    def tab(i):
        return (jnp.where(i < n_ctx_tiles, 0, 1 + (i - n_ctx_tiles) % tiles_per_seq), 0)

    full = lambda shape: pl.BlockSpec(shape, lambda i: (0,) * len(shape))
    row = lambda w: pl.BlockSpec((TM_IN, w), lambda i: (i, 0))
    return pl.pallas_call(
        _inproj_body,
        out_shape=(
            jax.ShapeDtypeStruct((T_ALL, ATTN_W), BF16),
            jax.ShapeDtypeStruct((T_ALL, KV_W), F32),
            jax.ShapeDtypeStruct((T_ALL, KV_W), F32),
            jax.ShapeDtypeStruct((T_ALL, CONV_W), BF16),
            jax.ShapeDtypeStruct((T_ALL, 2 * D), BF16),
        ),
        grid=(T_ALL // TM_IN,),
        in_specs=[
            row(D), full((MOD_ROWS, 6 * D)), full((1, D)), full((D, IN_W)),
            full((1, ATTN_W)), full((1, KV_W)),
            pl.BlockSpec((TM_IN, ATTN_W), tab), pl.BlockSpec((TM_IN, ATTN_W), tab),
            full((ATTN_W, ATTN_W)), full((ATTN_W, ATTN_W)),
        ],
        out_specs=(row(ATTN_W), row(KV_W), row(KV_W), row(CONV_W), row(2 * D)),
        compiler_params=pltpu.CompilerParams(dimension_semantics=("arbitrary",), vmem_limit_bytes=VMEM_LIMIT),
        name="inproj",
    )(x, mod_l, n1, w_in, qg, kg, cos_t, sin_t, bd, rm)


def _head_pair_kv(k, v):
    lane = lax.broadcasted_iota(jnp.int32, k.shape, 1)
    low = lane < HD
    zero = jnp.zeros_like(k)
    k_sw = pltpu.roll(k, HD, 1)
    v_sw = pltpu.roll(v, HD, 1)
    g0 = (jnp.where(low, k, zero), jnp.where(low, v, zero), jnp.where(low, zero, k_sw), jnp.where(low, zero, v_sw))
    g1 = (jnp.where(low, k_sw, zero), jnp.where(low, v_sw, zero), jnp.where(low, zero, k), jnp.where(low, zero, v))
    return tuple(tuple(t.astype(BF16) for t in g) for g in (g0, g1))


def _sink_attend(qp, kk, vv, sink, mask):
    s = lax.dot_general(qp, kk, (((1,), (1,)), ((), ())), preferred_element_type=F32)
    if mask is not None:
        s = jnp.where(mask, s, NEG)
    m = jnp.maximum(jnp.max(s, axis=-1, keepdims=True), sink)
    p = jnp.exp(s - m)
    den = jnp.sum(p, axis=-1, keepdims=True) + jnp.exp(sink - m)
    return jnp.dot(p.astype(BF16), vv, preferred_element_type=F32) / den


def _attend_all_heads(sink_ref, q_ref, o_ref, kv, mask):
    for pair in range(N_HEADS // 2):
        k_lo, v_lo, k_hi, v_hi = kv[pair // 2]
        qp = q_ref[:, pair * 128:(pair + 1) * 128]
        o = (_sink_attend(qp, k_lo, v_lo, sink_ref[2 * pair], mask)
             + _sink_attend(qp, k_hi, v_hi, sink_ref[2 * pair + 1], mask))
        o_ref[:, pair * 128:(pair + 1) * 128] = o.astype(BF16)


def _attn_ctx_body(sink_ref, q_ref, k_ref, v_ref, o_ref):
    _attend_all_heads(sink_ref, q_ref, o_ref, _head_pair_kv(k_ref[...], v_ref[...]), None)


def _attn_ctx(sink_l, q, k, v):
    blk = lambda w: pl.BlockSpec((CTX_L, w), lambda b: (b, 0))
    return pl.pallas_call(
        _attn_ctx_body,
        out_shape=jax.ShapeDtypeStruct((T_CTX, ATTN_W), BF16),
        grid=(CTX_B,),
        in_specs=[pl.BlockSpec(memory_space=pltpu.SMEM), blk(ATTN_W), blk(KV_W), blk(KV_W)],
        out_specs=blk(ATTN_W),
        compiler_params=pltpu.CompilerParams(dimension_semantics=("arbitrary",)),
        name="attn_ctx",
    )(sink_l, q, k, v)


def _attn_lat_body(sink_ref, q_ref, kp_ref, kc_ref, kn_ref, vp_ref, vc_ref, vn_ref, ck_ref, cv_ref, o_ref):
    i = pl.program_id(1)
    nblk = LAT_L // QBLK
    k = jnp.concatenate([kp_ref[...], kc_ref[...], kn_ref[...], ck_ref[0, 0]], axis=0)
    v = jnp.concatenate([vp_ref[...], vc_ref[...], vn_ref[...], cv_ref[0, 0]], axis=0)
    nk = 3 * QBLK + PAST
    r = lax.broadcasted_iota(jnp.int32, (QBLK, nk), 0)
    c = lax.broadcasted_iota(jnp.int32, (QBLK, nk), 1)
    local = (c - r >= 0) & (c - r <= 2 * WINDOW)
    local = local & ((c >= QBLK) | (i > 0)) & ((c < 2 * QBLK) | (i < nblk - 1))
    mask = local | (c >= 3 * QBLK)
    _attend_all_heads(sink_ref, q_ref, o_ref, _head_pair_kv(k, v), mask)


def _attn_lat(sink_l, q, k, v, cache_k, cache_v, layer):
    nblk = LAT_L // QBLK
    base = T_CTX // QBLK
    cur = lambda b, i: (base + b * nblk + i, 0)
    prev = lambda b, i: (base + b * nblk + jnp.maximum(i - 1, 0), 0)
    nxt = lambda b, i: (base + b * nblk + jnp.minimum(i + 1, nblk - 1), 0)
    kvb = lambda f: pl.BlockSpec((QBLK, KV_W), f)
    cache = pl.BlockSpec((1, 1, PAST, KV_W), lambda b, i: (b, layer, 0, 0))
    return pl.pallas_call(
        _attn_lat_body,
        out_shape=jax.ShapeDtypeStruct((T_LAT, ATTN_W), BF16),
        grid=(LAT_B, nblk),
        in_specs=[pl.BlockSpec(memory_space=pltpu.SMEM), pl.BlockSpec((QBLK, ATTN_W), cur),
                  kvb(prev), kvb(cur), kvb(nxt), kvb(prev), kvb(cur), kvb(nxt), cache, cache],
        out_specs=pl.BlockSpec((QBLK, ATTN_W), lambda b, i: (b * nblk + i, 0)),
        compiler_params=pltpu.CompilerParams(dimension_semantics=("arbitrary", "arbitrary")),
        name="attn_lat",
    )(sink_l, q, k, k, k, v, v, v, cache_k, cache_v)


def _conv_body(up_ref, uc_ref, un_ref, w_ref, b_ref, g_ref, beta_ref, o_ref, pad_ref):
    i = pl.program_id(0)
    n_ctx = T_CTX // TM_CONV
    per_seq = LAT_L // TM_CONV
    j = (i - n_ctx) % per_seq
    has_prev = (i >= n_ctx) & (j > 0)
    has_next = (i >= n_ctx) & (j < per_seq - 1)
    zero = jnp.zeros((CONV_HALO, CONV_W), F32)
    pad_ref[0, 0:CONV_HALO, :] = jnp.where(has_prev, up_ref[...].astype(F32), zero)
    pad_ref[0, CONV_HALO:CONV_HALO + TM_CONV, :] = uc_ref[...].astype(F32)
    pad_ref[0, CONV_HALO + TM_CONV:, :] = jnp.where(has_next, un_ref[...].astype(F32), zero)
    n_sh = TM_CONV + 2 * CONV_HALO - 8
    for s in range(1, 8):
        pad_ref[s, 0:n_sh, :] = pad_ref[0, s:s + n_sh, :]
    chunk = 64
    for c0 in range(0, TM_CONV, chunk):
        acc = jnp.zeros((chunk, CONV_W), F32) + b_ref[...]
        for t in range(CONV_K):
            src = CONV_HALO - CONV_PAD + t
            row = src - src % 8 + c0
            acc = acc + pad_ref[src % 8, row:row + chunk, :] * w_ref[t:t + 1, :]
        mu = jnp.mean(acc, axis=-1, keepdims=True)
        cen = acc - mu
        var = jnp.mean(cen * cen, axis=-1, keepdims=True)
        y = cen * lax.rsqrt(var + EPS) * g_ref[...] + beta_ref[...]
        o_ref[c0:c0 + chunk, :] = (y * _sigmoid(y)).astype(BF16)


def _conv(u, dw_w, dw_b, ln_g, ln_b):
    per = TM_CONV // CONV_HALO
    last = T_ALL // CONV_HALO - 1
    full = lambda shape: pl.BlockSpec(shape, lambda i: (0,) * len(shape))
    return pl.pallas_call(
        _conv_body,
        out_shape=jax.ShapeDtypeStruct((T_ALL, CONV_W), BF16),
        grid=(T_ALL // TM_CONV,),
        in_specs=[
            pl.BlockSpec((CONV_HALO, CONV_W), lambda i: (jnp.maximum(i * per - 1, 0), 0)),
            pl.BlockSpec((TM_CONV, CONV_W), lambda i: (i, 0)),
            pl.BlockSpec((CONV_HALO, CONV_W), lambda i: (jnp.minimum((i + 1) * per, last), 0)),
            full((CONV_K, CONV_W)), full((1, CONV_W)), full((1, CONV_W)), full((1, CONV_W)),
        ],
        out_specs=pl.BlockSpec((TM_CONV, CONV_W), lambda i: (i, 0)),
        scratch_shapes=[pltpu.VMEM((8, TM_CONV + 2 * CONV_HALO, CONV_W), F32)],
        compiler_params=pltpu.CompilerParams(dimension_semantics=("arbitrary",)),
        name="conv_module",
    )(u, u, u, dw_w, dw_b, ln_g, ln_b)


def _row_pick(idx, rows):
    out = jnp.zeros_like(rows[0])
    for e, r in enumerate(rows):
        out = jnp.where(idx == e, r, out)
    return out


def _argmax_rows(rows):
    best = rows[0]
    idx = jnp.zeros(rows[0].shape, jnp.int32)
    for e in range(1, len(rows)):
        upd = rows[e] > best
        idx = jnp.where(upd, e, idx)
        best = jnp.where(upd, rows[e], best)
    return idx


def _route(logits_t, bias):
    scores = _sigmoid(logits_t)
    sel = scores + bias
    s_rows = [scores[e:e + 1, :] for e in range(N_EXP)]
    rows = [sel[e:e + 1, :] for e in range(N_EXP)]
    grp = []
    for g in range(N_GROUPS):
        r = rows[g * EXP_PER_GROUP:(g + 1) * EXP_PER_GROUP]
        best = None
        for a in range(EXP_PER_GROUP):
            for b in range(a + 1, EXP_PER_GROUP):
                s = r[a] + r[b]
                best = s if best is None else jnp.maximum(best, s)
        grp.append(best)
    gidx = _argmax_rows(grp)
    masked = [jnp.where(gidx == e // EXP_PER_GROUP, rows[e], NEG) for e in range(N_EXP)]
    i1 = _argmax_rows(masked)
    i2 = _argmax_rows([jnp.where(i1 == e, -jnp.inf, masked[e]) for e in range(N_EXP)])
    w1 = _row_pick(i1, s_rows)
    w2 = _row_pick(i2, s_rows)
    tot = w1 + w2
    return i1, i2, w1 / tot, w2 / tot


def _merge_body(ac_ref, al_ref, c_ref, g_ref, x_ref, mod_ref, n2_ref, wap_ref, wcp_ref, wo_ref, wr_ref,
                x1_ref, h_ref, lg_ref):
    step = pl.program_id(0)
    row = _mod_row(step, TM_MERGE)
    g1 = mod_ref[pl.ds(row, 1), 2 * D:3 * D]
    sh2 = mod_ref[pl.ds(row, 1), 3 * D:4 * D]
    sc2 = mod_ref[pl.ds(row, 1), 4 * D:5 * D]
    a = jnp.where(step < T_CTX // TM_MERGE, ac_ref[...], al_ref[...])
    attn = jnp.dot(a, wap_ref[...], preferred_element_type=F32)
    conv = jnp.dot(c_ref[...], wcp_ref[...], preferred_element_type=F32)
    merged = g_ref[:, 0:D].astype(F32) * attn + g_ref[:, D:2 * D].astype(F32) * conv
    mix = jnp.dot(merged.astype(BF16), wo_ref[...], preferred_element_type=F32)
    x1 = x_ref[...] + g1 * mix
    x1_ref[...] = x1
    ms = jnp.mean(x1 * x1, axis=-1, keepdims=True)
    h = (x1 * lax.rsqrt(ms + EPS) * n2_ref[...]) * (1.0 + sc2) + sh2
    h_ref[...] = h.astype(BF16)
    lg_ref[...] = lax.dot_general(wr_ref[...], h, (((1,), (1,)), ((), ())),
                                  precision=lax.Precision.HIGHEST, preferred_element_type=F32)


def _merge(attn_ctx, attn_lat, conv_o, gates, x, mod_l, n2, wap, wcp, wo, wr_t):
    tm = TM_MERGE
    n_ctx = T_CTX // tm
    full = lambda shape: pl.BlockSpec(shape, lambda i: (0,) * len(shape))
    row = lambda w: pl.BlockSpec((tm, w), lambda i: (i, 0))
    a_ctx = pl.BlockSpec((tm, ATTN_W), lambda i: (jnp.minimum(i, n_ctx - 1), 0))
    a_lat = pl.BlockSpec((tm, ATTN_W), lambda i: (jnp.maximum(i - n_ctx, 0), 0))
    return pl.pallas_call(
        _merge_body,
        out_shape=(
            jax.ShapeDtypeStruct((T_ALL, D), F32),
            jax.ShapeDtypeStruct((T_ALL, D), BF16),
            jax.ShapeDtypeStruct((N_EXP, T_ALL), F32),
        ),
        grid=(T_ALL // tm,),
        in_specs=[a_ctx, a_lat, row(CONV_W), row(2 * D), row(D), full((MOD_ROWS, 6 * D)), full((1, D)),
                  full((ATTN_W, D)), full((CONV_W, D)), full((D, D)), full((N_EXP, D))],
        out_specs=(row(D), row(D), pl.BlockSpec((N_EXP, tm), lambda i: (0, i))),
        compiler_params=pltpu.CompilerParams(dimension_semantics=("arbitrary",), vmem_limit_bytes=VMEM_LIMIT),
        name="merge",
    )(attn_ctx, attn_lat, conv_o, gates, x, mod_l, n2, wap, wcp, wo, wr_t)


def _router_body(lg_ref, rb_ref, ri_ref, rw_ref, cnt_ref):
    tm = TM_ROUTE
    i1, i2, w1, w2 = _route(lg_ref[...], rb_ref[...])

    eio = lax.broadcasted_iota(jnp.int32, (N_EXP, tm), 0)
    oh1 = eio == i1
    oh2 = eio == i2
    oh = (oh1 | oh2).astype(F32)
    nc = ROUTE_CHUNK
    tri = (lax.broadcasted_iota(jnp.int32, (nc, nc), 0) < lax.broadcasted_iota(jnp.int32, (nc, nc), 1)).astype(BF16)
    lane = lax.broadcasted_iota(jnp.int32, (N_EXP, 128), 1)
    counts = jnp.zeros((N_EXP, 128), F32)
    pos = []
    for s0 in range(0, tm, SORT_TL):
        base = jnp.zeros((N_EXP, 1), F32)
        for c0 in range(s0, s0 + SORT_TL, nc):
            ohc = oh[:, c0:c0 + nc]
            pos.append(base + jnp.dot(ohc.astype(BF16), tri, preferred_element_type=F32))
            base = base + jnp.sum(ohc, axis=1, keepdims=True)
        counts = jnp.where(lane == s0 // SORT_TL, base, counts)
    pos = jnp.concatenate(pos, axis=1)
    r1 = jnp.sum(jnp.where(oh1, pos, 0.0), axis=0, keepdims=True).astype(jnp.int32)
    r2 = jnp.sum(jnp.where(oh2, pos, 0.0), axis=0, keepdims=True).astype(jnp.int32)
    cnt_ref[0] = counts

    sub = lax.broadcasted_iota(jnp.int32, (8, tm), 0)
    zi = jnp.zeros((8, tm), jnp.int32)
    ri_ref[...] = jnp.where(sub == 0, i1, jnp.where(sub == 1, i2, jnp.where(sub == 2, r1, jnp.where(sub == 3, r2, zi))))
    rw_ref[...] = jnp.where(sub == 0, w1, jnp.where(sub == 1, w2, jnp.zeros((8, tm), F32)))


def _router(logits_t, rb):
    tm = TM_ROUTE
    lanes = lambda rows: pl.BlockSpec((rows, tm), lambda i: (0, i))
    return pl.pallas_call(
        _router_body,
        out_shape=(
            jax.ShapeDtypeStruct((8, T_ALL), jnp.int32),
            jax.ShapeDtypeStruct((8, T_ALL), F32),
            jax.ShapeDtypeStruct((T_ALL // tm, N_EXP, 128), F32),
        ),
        grid=(T_ALL // tm,),
        in_specs=[lanes(N_EXP), pl.BlockSpec((N_EXP, 1), lambda i: (0, 0))],
        out_specs=(lanes(8), lanes(8), pl.BlockSpec((1, N_EXP, 128), lambda i: (i, 0, 0))),
        compiler_params=pltpu.CompilerParams(dimension_semantics=("arbitrary",)),
        name="router",
    )(logits_t, rb)


def _sort_tables(ri, cnt):
    per = TM_ROUTE // SORT_TL
    c = jnp.transpose(cnt[:, :, :per], (0, 2, 1)).reshape(N_SORT, N_EXP).astype(jnp.int32)
    pc = (c + (CHUNK - 1)) // CHUNK * CHUNK
    loff = jnp.cumsum(pc, axis=1) - pc
    used = jnp.sum(pc, axis=1)
    seg = jnp.sum(pc, axis=0)
    ends = jnp.cumsum(seg)
    starts = ends - seg
    gstart = starts[None, :] + jnp.cumsum(pc, axis=0) - pc
    loff_t = jnp.repeat(loff.T, SORT_TL, axis=1)
    eid = jnp.arange(N_EXP, dtype=jnp.int32)[None, :, None]
    lpos = jnp.sum(jnp.where(ri[0:2, None, :] == eid, loff_t[None], 0), axis=1) + ri[2:4]
    flat = lambda a: (a // CHUNK).reshape(-1).astype(jnp.int32)
    return (lpos.astype(jnp.int32), flat(pc), flat(loff), flat(gstart), (used // CHUNK).astype(jnp.int32),
            starts.astype(jnp.int32), ends.astype(jnp.int32))


def _run_copies(nch_ref, sch_ref, dch_ref, tile, make_copy, act):
    for e in range(N_EXP):
        n = nch_ref[tile * N_EXP + e]
        s0 = sch_ref[tile * N_EXP + e]
        d0 = dch_ref[tile * N_EXP + e]

        def body(j, carry, s0=s0, d0=d0):
            act(make_copy(s0 + j, d0 + j))
            return carry

        lax.fori_loop(0, n, body, 0)


def _sort_body(nch_ref, sch_ref, dch_ref, tail_ref, lp_ref, h_ref, xs_hbm, xl_ref, sem):
    i = pl.program_id(0)
    slot = i % 2
    last = pl.num_programs(0) - 1
    r = lax.broadcasted_iota(jnp.int32, (CAP, SORT_TL), 0)
    place = ((r == lp_ref[0:1, :]) | (r == lp_ref[1:2, :])).astype(BF16)
    xl_ref[slot] = jnp.dot(place, h_ref[...], preferred_element_type=F32).astype(BF16)

    def copy_from(buf):
        def make(src, dst):
            return pltpu.make_async_copy(xl_ref.at[buf, pl.ds(src * CHUNK, CHUNK), :],
                                         xs_hbm.at[pl.ds(dst * CHUNK, CHUNK), :], sem.at[buf])
        return make

    def tail_copies(act):
        def body(j, carry):
            act(copy_from(slot)(N_CHUNK - 1, tail_ref[0] + j))
            return carry
        lax.fori_loop(0, tail_ref[1], body, 0)

    _run_copies(nch_ref, sch_ref, dch_ref, i, copy_from(slot), lambda c: c.start())

    @pl.when(i == last)
    def _():
        tail_copies(lambda c: c.start())

    @pl.when(i > 0)
    def _():
        _run_copies(nch_ref, sch_ref, dch_ref, i - 1, copy_from(1 - slot), lambda c: c.wait())

    @pl.when(i == last)
    def _():
        _run_copies(nch_ref, sch_ref, dch_ref, i, copy_from(slot), lambda c: c.wait())
        tail_copies(lambda c: c.wait())


def _sort_scatter(nch, sch, dch, tail, lpos8, h):
    return pl.pallas_call(
        _sort_body,
        out_shape=jax.ShapeDtypeStruct((XS_ROWS, D), BF16),
        grid_spec=pltpu.PrefetchScalarGridSpec(
            num_scalar_prefetch=4,
            grid=(N_SORT,),
            in_specs=[pl.BlockSpec((8, SORT_TL), lambda i, *_: (0, i)),
                      pl.BlockSpec((SORT_TL, D), lambda i, *_: (i, 0))],
            out_specs=pl.BlockSpec(memory_space=pl.ANY),
            scratch_shapes=[pltpu.VMEM((2, CAP, D), BF16), pltpu.SemaphoreType.DMA((2,))],
        ),
        compiler_params=pltpu.CompilerParams(dimension_semantics=("arbitrary",), vmem_limit_bytes=VMEM_LIMIT),
        name="sort_scatter",
    )(nch, sch, dch, tail, lpos8, h)


def _ffn_body(tile_ref, exp_ref, lo_ref, hi_ref, first_ref, mode_ref, fresh_ref,
              x_ref, wg_ref, wu_ref, wd_ref, o_ref, wg_b, wu_b, wd_b):
    u = pl.program_id(0)

    @pl.when(mode_ref[u] == 2)
    def _():
        o_ref[...] = jnp.zeros_like(o_ref)

    @pl.when(fresh_ref[u] == 1)
    def _():
        wg_b[...] = wg_ref[0, 0].astype(BF16)
        wu_b[...] = wu_ref[0, 0].astype(BF16)
        wd_b[...] = wd_ref[0, 0].astype(BF16)

    @pl.when(mode_ref[u] == 1)
    def _():
        x = x_ref[...]
        g = jnp.dot(x, wg_b[...], preferred_element_type=F32)
        up = jnp.dot(x, wu_b[...], preferred_element_type=F32)
        rows = lax.broadcasted_iota(jnp.int32, (TM_FFN, 1), 0)
        mine = (rows >= lo_ref[u]) & (rows < hi_ref[u])
        hid = jnp.where(mine, g * _sigmoid(g) * up, 0.0).astype(BF16)
        y = jnp.dot(hid, wd_b[...], preferred_element_type=F32).astype(BF16)

        @pl.when(first_ref[u] == 1)
        def _():
            o_ref[...] = y

        @pl.when(first_ref[u] == 0)
        def _():
            o_ref[...] += y


def _ffn(units, xs, wg, wu, wd, layer):
    xmap = lambda u, tile, *_: (tile[u], 0)
    wmap = lambda u, tile, exp, *_: (layer, exp[u], 0, 0)
    return pl.pallas_call(
        _ffn_body,
        out_shape=jax.ShapeDtypeStruct((XS_ROWS, D), BF16),
        grid_spec=pltpu.PrefetchScalarGridSpec(
            num_scalar_prefetch=7,
            grid=(N_UNITS,),
            in_specs=[pl.BlockSpec((TM_FFN, D), xmap),
                      pl.BlockSpec((1, 1, D, D_EXP), wmap), pl.BlockSpec((1, 1, D, D_EXP), wmap),
                      pl.BlockSpec((1, 1, D_EXP, D), wmap)],
            out_specs=pl.BlockSpec((TM_FFN, D), xmap),
            scratch_shapes=[pltpu.VMEM((D, D_EXP), BF16), pltpu.VMEM((D, D_EXP), BF16), pltpu.VMEM((D_EXP, D), BF16)],
        ),
        compiler_params=pltpu.CompilerParams(dimension_semantics=("arbitrary",), vmem_limit_bytes=VMEM_LIMIT),
        name="expert_ffn",
    )(*units, xs, wg, wu, wd)


def _ffn_units(starts, ends):
    total_rows = ends[-1]
    t0 = jnp.arange(N_FFN_TILES, dtype=jnp.int32) * TM_FFN
    t1 = jnp.minimum(t0 + TM_FFN, total_rows) - 1
    e_first = jnp.sum(ends[None, :] <= t0[:, None], axis=1).astype(jnp.int32)
    e_last = jnp.sum(ends[None, :] <= t1[:, None], axis=1).astype(jnp.int32)
    n_per = jnp.where(t0 < total_rows, e_last - e_first + 1, 0)
    u_end = jnp.cumsum(n_per)
    u_start = u_end - n_per
    total = u_end[-1]
    u = jnp.arange(N_UNITS, dtype=jnp.int32)
    uc = jnp.minimum(u, total - 1)
    tile = jnp.sum(u_end[None, :] <= uc[:, None], axis=1).astype(jnp.int32)
    exp = e_first[tile] + (uc - u_start[tile])
    lo = jnp.clip(starts[exp] - tile * TM_FFN, 0, TM_FFN)
    hi = jnp.clip(ends[exp] - tile * TM_FFN, 0, TM_FFN)
    first = ((uc == u_start[tile]) & (u < total)).astype(jnp.int32)
    exp = exp.astype(jnp.int32)
    fresh = jnp.concatenate([jnp.ones((1,), jnp.int32), (exp[1:] != exp[:-1]).astype(jnp.int32)])
    spare = jnp.sum(t0 < total_rows).astype(jnp.int32) + (u - total)
    mode = jnp.where(u < total, 1, jnp.where(spare < N_FFN_TILES, 2, 0)).astype(jnp.int32)
    tile = jnp.where(u < total, tile, jnp.minimum(spare, N_FFN_TILES - 1)).astype(jnp.int32)
    return tile, exp, lo.astype(jnp.int32), hi.astype(jnp.int32), first, mode, fresh


def _combine_body(nch_ref, sch_ref, dch_ref, used_ref, lp_ref, w_ref, x1_ref, mod_ref, ys_hbm, o_ref, yl_ref, sem):
    i = pl.program_id(0)
    slot = i % 2
    last = pl.num_programs(0) - 1

    def copy_into(buf):
        def make(loc, glob):
            return pltpu.make_async_copy(ys_hbm.at[pl.ds(glob * CHUNK, CHUNK), :],
                                         yl_ref.at[buf, pl.ds(loc * CHUNK, CHUNK), :], sem.at[buf])
        return make

    @pl.when(i == 0)
    def _():
        _run_copies(nch_ref, sch_ref, dch_ref, i, copy_into(slot), lambda c: c.start())

    @pl.when(i < last)
    def _():
        _run_copies(nch_ref, sch_ref, dch_ref, i + 1, copy_into(1 - slot), lambda c: c.start())

    row = _mod_row(i, SORT_TL)
    g2 = mod_ref[pl.ds(row, 1), 5 * D:6 * D]
    lane = lax.broadcasted_iota(jnp.int32, (SORT_TL, CAP), 1)
    comb = (jnp.where(lane == lp_ref[:, 0:1], w_ref[:, 0:1], 0.0)
            + jnp.where(lane == lp_ref[:, 1:2], w_ref[:, 1:2], 0.0)).astype(BF16)

    _run_copies(nch_ref, sch_ref, dch_ref, i, copy_into(slot), lambda c: c.wait())

    def clear(j, carry):
        yl_ref[slot, pl.ds(pl.multiple_of(j * CHUNK, CHUNK), CHUNK), :] = jnp.zeros((CHUNK, D), BF16)
        return carry

    lax.fori_loop(used_ref[i], N_CHUNK, clear, 0)
    y = jnp.dot(comb, yl_ref[slot], preferred_element_type=F32)
    o_ref[...] = x1_ref[...] + g2 * y


def _combine(nch, sch, dch, used, lpos_tok, w_tok, x1, mod_l, ys):
    tok = lambda w: pl.BlockSpec((SORT_TL, w), lambda i, *_: (i, 0))
    return pl.pallas_call(
        _combine_body,
        out_shape=jax.ShapeDtypeStruct((T_ALL, D), F32),
        grid_spec=pltpu.PrefetchScalarGridSpec(
            num_scalar_prefetch=4,
            grid=(N_SORT,),
            in_specs=[tok(2), tok(2), tok(D),
                      pl.BlockSpec((MOD_ROWS, 6 * D), lambda i, *_: (0, 0)),
                      pl.BlockSpec(memory_space=pl.ANY)],
            out_specs=tok(D),
            scratch_shapes=[pltpu.VMEM((2, CAP, D), BF16), pltpu.SemaphoreType.DMA((2,))],
        ),
        compiler_params=pltpu.CompilerParams(dimension_semantics=("arbitrary",), vmem_limit_bytes=VMEM_LIMIT),
        name="combine",
    )(nch, sch, dch, used, lpos_tok, w_tok, x1, mod_l, ys)


def _final_norm_body(x_ref, g_ref, o_ref):
    x = x_ref[...]
    ms = jnp.mean(x * x, axis=-1, keepdims=True)
    o_ref[...] = x * lax.rsqrt(ms + EPS) * g_ref[...]


def _final_norm(x, gain, row0, rows):
    tm = 512
    first = row0 // tm
    return pl.pallas_call(
        _final_norm_body,
        out_shape=jax.ShapeDtypeStruct((rows, D), F32),
        grid=(rows // tm,),
        in_specs=[pl.BlockSpec((tm, D), lambda i: (first + i, 0)), pl.BlockSpec((1, D), lambda i: (0, 0))],
        out_specs=pl.BlockSpec((tm, D), lambda i: (i, 0)),
        name="final_norm",
    )(x, gain)


def _rope_tables():
    pos = jnp.arange(LAT_L)
    rowp = (pos // GRID_W).astype(F32)
    colp = (pos % GRID_W).astype(F32)
    pairs = HD // 4
    inv = ROPE_BASE ** (-jnp.arange(pairs, dtype=F32) / pairs)
    ang = jnp.concatenate([rowp[:, None] * inv] * 2 + [colp[:, None] * inv] * 2, axis=-1)
    cos = jnp.tile(jnp.cos(ang), (1, N_HEADS))
    sin = jnp.tile(jnp.sin(ang), (1, N_HEADS))
    cos = jnp.concatenate([jnp.ones((TM_IN, ATTN_W), F32), cos], axis=0)
    sin = jnp.concatenate([jnp.zeros((TM_IN, ATTN_W), F32), sin], axis=0)
    return cos, sin


def _head_matrices():
    i = jnp.arange(ATTN_W)
    bd = (i[:, None] // HD == i[None, :] // HD).astype(BF16)
    half = HD // 4
    j = i[None, :]
    src = i[:, None]
    first = (j % (2 * half)) < half
    rm = jnp.where(first & (src == j + half), -1.0, 0.0) + jnp.where(~first & (src == j - half), 1.0, 0.0)
    return bd, rm.astype(BF16)


def kernel(x_prompt, x_sample, cache_k, cache_v, c, c_ctx, w_ada, b_ada, norm1, norm2, w_in, q_norm, k_norm,
           sink, w_attn_proj, dw_w, dw_b, cln_g, cln_b, w_conv_proj, w_out, w_router, router_bias,
           w_e_gate, w_e_up, w_e_down, final_norm):
    x = jnp.concatenate([x_prompt.reshape(T_CTX, D), x_sample.reshape(T_LAT, D)], axis=0)
    cond = jnp.concatenate([c_ctx[None, :], c, jnp.zeros((MOD_ROWS - 1 - LAT_B, D), F32)], axis=0)
    mod = _ada_table(cond, w_ada, b_ada)
    cos_t, sin_t = _rope_tables()
    bd, rm = _head_matrices()
    w_in_b = w_in.astype(BF16)
    wap_b = w_attn_proj.astype(BF16)
    wcp_b = w_conv_proj.astype(BF16)
    wo_b = w_out.astype(BF16)
    wr_t = w_router.T
    rb = router_bias.reshape(N_EXP, 1)
    ck = cache_k.reshape(LAT_B, DEPTH, PAST, KV_W)
    cv = cache_v.reshape(LAT_B, DEPTH, PAST, KV_W)

    new_k, new_v = [], []
    for l in range(DEPTH):
        q, k, v, u, gates = _inproj(x, mod[l], norm1[l][None, :], w_in_b[l],
                                    jnp.tile(q_norm[l], N_HEADS)[None, :], jnp.tile(k_norm[l], N_KV)[None, :],
                                    cos_t, sin_t, bd, rm)
        new_k.append(k[:T_CTX].reshape(CTX_B, CTX_L, N_KV, HD))
        new_v.append(v[:T_CTX].reshape(CTX_B, CTX_L, N_KV, HD))
        o_ctx = _attn_ctx(sink[l], q, k, v)
        o_lat = _attn_lat(sink[l], q, k, v, ck, cv, l)
        conv_o = _conv(u, dw_w[l], dw_b[l][None, :], cln_g[l][None, :], cln_b[l][None, :])
        x1, h, logits_t = _merge(o_ctx, o_lat, conv_o, gates, x, mod[l], norm2[l][None, :],
                                 wap_b[l], wcp_b[l], wo_b[l], wr_t)
        ri, rw, cnt = _router(logits_t, rb)
        lpos, nch, sch, dch, used, starts, ends = _sort_tables(ri, cnt)
        rows_ch = ends[-1] // CHUNK
        tail = jnp.stack([rows_ch, XS_ROWS // CHUNK - rows_ch]).astype(jnp.int32)
        lpos8 = jnp.concatenate([lpos, jnp.zeros((6, T_ALL), jnp.int32)], axis=0)
        xs = _sort_scatter(nch, sch, dch, tail, lpos8, h)
        ys = _ffn(_ffn_units(starts, ends), xs, w_e_gate, w_e_up, w_e_down, l)
        x = _combine(nch, sch, dch, used, lpos.T, rw[0:2].T, x1, mod[l], ys)

    gain = final_norm[None, :]
    y_prompt = _final_norm(x, gain, 0, T_CTX).reshape(CTX_B, CTX_L, D)
    y_sample = _final_norm(x, gain, T_CTX, T_LAT).reshape(LAT_B, LAT_L, D)
    return y_prompt, y_sample, jnp.stack(new_k, axis=1), jnp.stack(new_v, axis=1)
```

```python
import jax
import jax.numpy as jnp
from jax import lax
from jax.experimental import pallas as pl
from jax.experimental.pallas import tpu as pltpu

F32 = jnp.float32
BF16 = jnp.bfloat16

D = 1024
DEPTH = 4
CTX_B, CTX_L = 32, 256
LAT_B, LAT_L = 8, 1024
PAST = 512
T_CTX = CTX_B * CTX_L
T_LAT = LAT_B * LAT_L
T_ALL = T_CTX + T_LAT
GRID_W = 64
HD = 64
N_HEADS = 8
N_KV = 2
ATTN_W = N_HEADS * HD
KV_W = N_KV * HD
WINDOW = 128
QBLK = 128
CONV_W = 512
CONV_K = 31
CONV_PAD = CONV_K // 2
N_EXP = 16
N_GROUPS = 4
EXP_PER_GROUP = N_EXP // N_GROUPS
D_EXP = 512
IN_W = ATTN_W + 2 * KV_W + 2 * CONV_W + 2 * D
OFF_K = ATTN_W
OFF_V = ATTN_W + KV_W
OFF_A = ATTN_W + 2 * KV_W
OFF_B = OFF_A + CONV_W
OFF_G = OFF_A + 2 * CONV_W
EPS = 1e-6
NEG = -1e30
ROPE_BASE = 10000.0
MOD_ROWS = 16

TM_IN = 512
TM_MERGE = 512
TM_ROUTE = 2048
ROUTE_CHUNK = 256
TM_CONV = 256
CONV_HALO = 16
SORT_TL = 512
N_SORT = T_ALL // SORT_TL
CHUNK = 16
CAP = 2 * SORT_TL + 2 * 128
N_CHUNK = CAP // CHUNK
TM_FFN = 512
FFN_SUB = 256
XS_ROWS = 2 * T_ALL + N_SORT * N_EXP * (CHUNK - 1)
N_FFN_TILES = XS_ROWS // TM_FFN
N_UNITS = N_FFN_TILES + N_EXP - 1

VMEM_LIMIT = 56 * 1024 * 1024


def _sigmoid(x):
    return 1.0 / (1.0 + jnp.exp(-x))


def _mod_row(tile, tm):
    start = tile * tm
    return jnp.where(start < T_CTX, 0, 1 + (start - T_CTX) // LAT_L)


def _ada_body(cond_ref, w_ref, b_ref, o_ref):
    c = cond_ref[...]
    s = (c * _sigmoid(c)).astype(BF16)
    o_ref[0] = jnp.dot(s, w_ref[0].astype(BF16), preferred_element_type=F32) + b_ref[0]


def _ada_table(cond, w_ada, b_ada):
    nj = 6 * D // 1024
    return pl.pallas_call(
        _ada_body,
        out_shape=jax.ShapeDtypeStruct((DEPTH, MOD_ROWS, 6 * D), F32),
        grid=(DEPTH, nj),
        in_specs=[
            pl.BlockSpec((MOD_ROWS, D), lambda l, j: (0, 0)),
            pl.BlockSpec((1, D, 1024), lambda l, j: (l, 0, j)),
            pl.BlockSpec((1, 1, 1024), lambda l, j: (l, 0, j)),
        ],
        out_specs=pl.BlockSpec((1, MOD_ROWS, 1024), lambda l, j: (l, 0, j)),
        name="ada_table",
    )(cond, w_ada, b_ada.reshape(DEPTH, 1, 6 * D))


def _head_norm_rope(y, gain, cos, sin, bd, rm):
    ss = jnp.dot((y * y).astype(BF16), bd, preferred_element_type=F32)
    yn = y * lax.rsqrt(ss * (1.0 / HD) + EPS) * gain
    rot = jnp.dot(yn.astype(BF16), rm, preferred_element_type=F32)
    return yn * cos + rot * sin


def _inproj_body(x_ref, mod_ref, n1_ref, w_ref, qg_ref, kg_ref, cos_ref, sin_ref, bd_ref, rm_ref,
                 q_ref, k_ref, v_ref, u_ref, g_ref):
    row = _mod_row(pl.program_id(0), TM_IN)
    sh = mod_ref[pl.ds(row, 1), 0:D]
    sc = mod_ref[pl.ds(row, 1), D:2 * D]
    x = x_ref[...]
    ms = jnp.mean(x * x, axis=-1, keepdims=True)
    h = ((x * lax.rsqrt(ms + EPS) * n1_ref[...]) * (1.0 + sc) + sh).astype(BF16)

    def proj(lo, hi):
        return jnp.dot(h, w_ref[:, lo:hi], preferred_element_type=F32)

    cos = cos_ref[...]
    sin = sin_ref[...]
    q = _head_norm_rope(proj(0, OFF_K), qg_ref[...], cos, sin, bd_ref[...], rm_ref[...])
    q_ref[...] = (q * (HD ** -0.5)).astype(BF16)
    k = _head_norm_rope(proj(OFF_K, OFF_V), kg_ref[...], cos[:, :KV_W], sin[:, :KV_W],
                        bd_ref[:KV_W, :KV_W], rm_ref[:KV_W, :KV_W])
    k_ref[...] = k
    v_ref[...] = proj(OFF_V, OFF_A)
    a = proj(OFF_A, OFF_B)
    b = proj(OFF_B, OFF_G)
    u_ref[...] = (a * _sigmoid(b)).astype(BF16)
    for j in range(2):
        g = proj(OFF_G + j * D, OFF_G + (j + 1) * D)
        g_ref[:, j * D:(j + 1) * D] = _sigmoid(g).astype(BF16)


def _inproj(x, mod_l, n1, w_in, qg, kg, cos_t, sin_t, bd, rm):
    n_ctx_tiles = T_CTX // TM_IN
    tiles_per_seq = LAT_L // TM_IN

    def tab(i):
        return (jnp.where(i < n_ctx_tiles, 0, 1 + (i - n_ctx_tiles) % tiles_per_seq), 0)

    full = lambda shape: pl.BlockSpec(shape, lambda i: (0,) * len(shape))
    row = lambda w: pl.BlockSpec((TM_IN, w), lambda i: (i, 0))
    return pl.pallas_call(
        _inproj_body,
        out_shape=(
            jax.ShapeDtypeStruct((T_ALL, ATTN_W), BF16),
            jax.ShapeDtypeStruct((T_ALL, KV_W), F32),
            jax.ShapeDtypeStruct((T_ALL, KV_W), F32),
            jax.ShapeDtypeStruct((T_ALL, CONV_W), BF16),
            jax.ShapeDtypeStruct((T_ALL, 2 * D), BF16),
        ),
        grid=(T_ALL // TM_IN,),
        in_specs=[
            row(D), full((MOD_ROWS, 6 * D)), full((1, D)), full((D, IN_W)),
            full((1, ATTN_W)), full((1, KV_W)),
            pl.BlockSpec((TM_IN, ATTN_W), tab), pl.BlockSpec((TM_IN, ATTN_W), tab),
            full((ATTN_W, ATTN_W)), full((ATTN_W, ATTN_W)),
        ],
        out_specs=(row(ATTN_W), row(KV_W), row(KV_W), row(CONV_W), row(2 * D)),
        compiler_params=pltpu.CompilerParams(dimension_semantics=("arbitrary",), vmem_limit_bytes=VMEM_LIMIT),
        name="inproj",
    )(x, mod_l, n1, w_in, qg, kg, cos_t, sin_t, bd, rm)


def _head_pair_kv(k, v):
    lane = lax.broadcasted_iota(jnp.int32, k.shape, 1)
    low = lane < HD
    zero = jnp.zeros_like(k)
    k_sw = pltpu.roll(k, HD, 1)
    v_sw = pltpu.roll(v, HD, 1)
    g0 = (jnp.where(low, k, zero), jnp.where(low, v, zero), jnp.where(low, zero, k_sw), jnp.where(low, zero, v_sw))
    g1 = (jnp.where(low, k_sw, zero), jnp.where(low, v_sw, zero), jnp.where(low, zero, k), jnp.where(low, zero, v))
    return tuple(tuple(t.astype(BF16) for t in g) for g in (g0, g1))


def _sink_attend(qp, kk, vv, sink, mask):
    s = lax.dot_general(qp, kk, (((1,), (1,)), ((), ())), preferred_element_type=F32)
    if mask is not None:
        s = jnp.where(mask, s, NEG)
    m = jnp.maximum(jnp.max(s, axis=-1, keepdims=True), sink)
    p = jnp.exp(s - m)
    den = jnp.sum(p, axis=-1, keepdims=True) + jnp.exp(sink - m)
    return jnp.dot(p.astype(BF16), vv, preferred_element_type=F32) / den


def _attend_all_heads(sink_ref, q_ref, o_ref, kv, mask):
    nq = q_ref.shape[0]
    if mask is not None:
        mask = jnp.concatenate([mask, mask], axis=0)
    for g in range(N_KV):
        k_lo, v_lo, k_hi, v_hi = kv[g]
        q2 = jnp.concatenate([q_ref[:, (2 * g) * 128:(2 * g + 1) * 128],
                              q_ref[:, (2 * g + 1) * 128:(2 * g + 2) * 128]], axis=0)

        def sinks(parity):
            top = jnp.full((nq, 1), sink_ref[4 * g + parity], F32)
            bot = jnp.full((nq, 1), sink_ref[4 * g + 2 + parity], F32)
            return jnp.concatenate([top, bot], axis=0)

        o = _sink_attend(q2, k_lo, v_lo, sinks(0), mask) + _sink_attend(q2, k_hi, v_hi, sinks(1), mask)
        o_ref[:, (2 * g) * 128:(2 * g + 1) * 128] = o[:nq].astype(BF16)
        o_ref[:, (2 * g + 1) * 128:(2 * g + 2) * 128] = o[nq:].astype(BF16)


def _attn_ctx_body(sink_ref, q_ref, k_ref, v_ref, o_ref):
    _attend_all_heads(sink_ref, q_ref, o_ref, _head_pair_kv(k_ref[...], v_ref[...]), None)


def _attn_ctx(sink_l, q, k, v):
    blk = lambda w: pl.BlockSpec((CTX_L, w), lambda b: (b, 0))
    return pl.pallas_call(
        _attn_ctx_body,
        out_shape=jax.ShapeDtypeStruct((T_CTX, ATTN_W), BF16),
        grid=(CTX_B,),
        in_specs=[pl.BlockSpec(memory_space=pltpu.SMEM), blk(ATTN_W), blk(KV_W), blk(KV_W)],
        out_specs=blk(ATTN_W),
        compiler_params=pltpu.CompilerParams(dimension_semantics=("arbitrary",)),
        name="attn_ctx",
    )(sink_l, q, k, v)


def _attn_lat_body(sink_ref, q_ref, kp_ref, kc_ref, kn_ref, vp_ref, vc_ref, vn_ref, ck_ref, cv_ref, o_ref):
    i = pl.program_id(1)
    nblk = LAT_L // QBLK
    k = jnp.concatenate([kp_ref[...], kc_ref[...], kn_ref[...], ck_ref[0, 0]], axis=0)
    v = jnp.concatenate([vp_ref[...], vc_ref[...], vn_ref[...], cv_ref[0, 0]], axis=0)
    nk = 3 * QBLK + PAST
    r = lax.broadcasted_iota(jnp.int32, (QBLK, nk), 0)
    c = lax.broadcasted_iota(jnp.int32, (QBLK, nk), 1)
    local = (c - r >= 0) & (c - r <= 2 * WINDOW)
    local = local & ((c >= QBLK) | (i > 0)) & ((c < 2 * QBLK) | (i < nblk - 1))
    mask = local | (c >= 3 * QBLK)
    _attend_all_heads(sink_ref, q_ref, o_ref, _head_pair_kv(k, v), mask)


def _attn_lat(sink_l, q, k, v, cache_k, cache_v, layer):
    nblk = LAT_L // QBLK
    base = T_CTX // QBLK
    cur = lambda b, i: (base + b * nblk + i, 0)
    prev = lambda b, i: (base + b * nblk + jnp.maximum(i - 1, 0), 0)
    nxt = lambda b, i: (base + b * nblk + jnp.minimum(i + 1, nblk - 1), 0)
    kvb = lambda f: pl.BlockSpec((QBLK, KV_W), f)
    cache = pl.BlockSpec((1, 1, PAST, KV_W), lambda b, i: (b, layer, 0, 0))
    return pl.pallas_call(
        _attn_lat_body,
        out_shape=jax.ShapeDtypeStruct((T_LAT, ATTN_W), BF16),
        grid=(LAT_B, nblk),
        in_specs=[pl.BlockSpec(memory_space=pltpu.SMEM), pl.BlockSpec((QBLK, ATTN_W), cur),
                  kvb(prev), kvb(cur), kvb(nxt), kvb(prev), kvb(cur), kvb(nxt), cache, cache],
        out_specs=pl.BlockSpec((QBLK, ATTN_W), lambda b, i: (b * nblk + i, 0)),
        compiler_params=pltpu.CompilerParams(dimension_semantics=("arbitrary", "arbitrary")),
        name="attn_lat",
    )(sink_l, q, k, k, k, v, v, v, cache_k, cache_v)


def _conv_body(up_ref, uc_ref, un_ref, w_ref, b_ref, g_ref, beta_ref, o_ref, pad_ref):
    i = pl.program_id(0)
    n_ctx = T_CTX // TM_CONV
    per_seq = LAT_L // TM_CONV
    j = (i - n_ctx) % per_seq
    has_prev = (i >= n_ctx) & (j > 0)
    has_next = (i >= n_ctx) & (j < per_seq - 1)
    zero = jnp.zeros((CONV_HALO, CONV_W), F32)
    pad_ref[0, 0:CONV_HALO, :] = jnp.where(has_prev, up_ref[...].astype(F32), zero)
    pad_ref[0, CONV_HALO:CONV_HALO + TM_CONV, :] = uc_ref[...].astype(F32)
    pad_ref[0, CONV_HALO + TM_CONV:, :] = jnp.where(has_next, un_ref[...].astype(F32), zero)
    n_sh = TM_CONV + 2 * CONV_HALO - 8
    for s in range(1, 8):
        pad_ref[s, 0:n_sh, :] = pad_ref[0, s:s + n_sh, :]
    chunk = 64
    for c0 in range(0, TM_CONV, chunk):
        acc = jnp.zeros((chunk, CONV_W), F32) + b_ref[...]
        for t in range(CONV_K):
            src = CONV_HALO - CONV_PAD + t
            row = src - src % 8 + c0
            acc = acc + pad_ref[src % 8, row:row + chunk, :] * w_ref[t:t + 1, :]
        mu = jnp.mean(acc, axis=-1, keepdims=True)
        cen = acc - mu
        var = jnp.mean(cen * cen, axis=-1, keepdims=True)
        y = cen * lax.rsqrt(var + EPS) * g_ref[...] + beta_ref[...]
        o_ref[c0:c0 + chunk, :] = (y * _sigmoid(y)).astype(BF16)


def _conv(u, dw_w, dw_b, ln_g, ln_b):
    per = TM_CONV // CONV_HALO
    last = T_ALL // CONV_HALO - 1
    full = lambda shape: pl.BlockSpec(shape, lambda i: (0,) * len(shape))
    return pl.pallas_call(
        _conv_body,
        out_shape=jax.ShapeDtypeStruct((T_ALL, CONV_W), BF16),
        grid=(T_ALL // TM_CONV,),
        in_specs=[
            pl.BlockSpec((CONV_HALO, CONV_W), lambda i: (jnp.maximum(i * per - 1, 0), 0)),
            pl.BlockSpec((TM_CONV, CONV_W), lambda i: (i, 0)),
            pl.BlockSpec((CONV_HALO, CONV_W), lambda i: (jnp.minimum((i + 1) * per, last), 0)),
            full((CONV_K, CONV_W)), full((1, CONV_W)), full((1, CONV_W)), full((1, CONV_W)),
        ],
        out_specs=pl.BlockSpec((TM_CONV, CONV_W), lambda i: (i, 0)),
        scratch_shapes=[pltpu.VMEM((8, TM_CONV + 2 * CONV_HALO, CONV_W), F32)],
        compiler_params=pltpu.CompilerParams(dimension_semantics=("arbitrary",)),
        name="conv_module",
    )(u, u, u, dw_w, dw_b, ln_g, ln_b)


def _row_pick(idx, rows):
    out = jnp.zeros_like(rows[0])
    for e, r in enumerate(rows):
        out = jnp.where(idx == e, r, out)
    return out


def _argmax_rows(rows):
    best = rows[0]
    idx = jnp.zeros(rows[0].shape, jnp.int32)
    for e in range(1, len(rows)):
        upd = rows[e] > best
        idx = jnp.where(upd, e, idx)
        best = jnp.where(upd, rows[e], best)
    return idx


def _route(logits_t, bias):
    scores = _sigmoid(logits_t)
    sel = scores + bias
    s_rows = [scores[e:e + 1, :] for e in range(N_EXP)]
    rows = [sel[e:e + 1, :] for e in range(N_EXP)]
    grp = []
    for g in range(N_GROUPS):
        r = rows[g * EXP_PER_GROUP:(g + 1) * EXP_PER_GROUP]
        best = None
        for a in range(EXP_PER_GROUP):
            for b in range(a + 1, EXP_PER_GROUP):
                s = r[a] + r[b]
                best = s if best is None else jnp.maximum(best, s)
        grp.append(best)
    gidx = _argmax_rows(grp)
    masked = [jnp.where(gidx == e // EXP_PER_GROUP, rows[e], NEG) for e in range(N_EXP)]
    i1 = _argmax_rows(masked)
    i2 = _argmax_rows([jnp.where(i1 == e, -jnp.inf, masked[e]) for e in range(N_EXP)])
    w1 = _row_pick(i1, s_rows)
    w2 = _row_pick(i2, s_rows)
    tot = w1 + w2
    return i1, i2, w1 / tot, w2 / tot


def _merge_body(ac_ref, al_ref, c_ref, g_ref, x_ref, mod_ref, n2_ref, wap_ref, wcp_ref, wo_ref, wr_ref,
                x1_ref, h_ref, lg_ref):
    step = pl.program_id(0)
    row = _mod_row(step, TM_MERGE)
    g1 = mod_ref[pl.ds(row, 1), 2 * D:3 * D]
    sh2 = mod_ref[pl.ds(row, 1), 3 * D:4 * D]
    sc2 = mod_ref[pl.ds(row, 1), 4 * D:5 * D]
    a = jnp.where(step < T_CTX // TM_MERGE, ac_ref[...], al_ref[...])
    attn = jnp.dot(a, wap_ref[...], preferred_element_type=F32)
    conv = jnp.dot(c_ref[...], wcp_ref[...], preferred_element_type=F32)
    merged = g_ref[:, 0:D].astype(F32) * attn + g_ref[:, D:2 * D].astype(F32) * conv
    mix = jnp.dot(merged.astype(BF16), wo_ref[...], preferred_element_type=F32)
    x1 = x_ref[...] + g1 * mix
    x1_ref[...] = x1
    ms = jnp.mean(x1 * x1, axis=-1, keepdims=True)
    h = (x1 * lax.rsqrt(ms + EPS) * n2_ref[...]) * (1.0 + sc2) + sh2
    h_ref[...] = h.astype(BF16)
    lg_ref[...] = lax.dot_general(wr_ref[...], h, (((1,), (1,)), ((), ())),
                                  precision=lax.Precision.HIGHEST, preferred_element_type=F32)


def _merge(attn_ctx, attn_lat, conv_o, gates, x, mod_l, n2, wap, wcp, wo, wr_t):
    tm = TM_MERGE
    n_ctx = T_CTX // tm
    full = lambda shape: pl.BlockSpec(shape, lambda i: (0,) * len(shape))
    row = lambda w: pl.BlockSpec((tm, w), lambda i: (i, 0))
    a_ctx = pl.BlockSpec((tm, ATTN_W), lambda i: (jnp.minimum(i, n_ctx - 1), 0))
    a_lat = pl.BlockSpec((tm, ATTN_W), lambda i: (jnp.maximum(i - n_ctx, 0), 0))
    return pl.pallas_call(
        _merge_body,
        out_shape=(
            jax.ShapeDtypeStruct((T_ALL, D), F32),
            jax.ShapeDtypeStruct((T_ALL, D), BF16),
            jax.ShapeDtypeStruct((N_EXP, T_ALL), F32),
        ),
        grid=(T_ALL // tm,),
        in_specs=[a_ctx, a_lat, row(CONV_W), row(2 * D), row(D), full((MOD_ROWS, 6 * D)), full((1, D)),
                  full((ATTN_W, D)), full((CONV_W, D)), full((D, D)), full((N_EXP, D))],
        out_specs=(row(D), row(D), pl.BlockSpec((N_EXP, tm), lambda i: (0, i))),
        compiler_params=pltpu.CompilerParams(dimension_semantics=("arbitrary",), vmem_limit_bytes=VMEM_LIMIT),
        name="merge",
    )(attn_ctx, attn_lat, conv_o, gates, x, mod_l, n2, wap, wcp, wo, wr_t)


def _router_body(lg_ref, rb_ref, ri_ref, rw_ref, cnt_ref):
    tm = TM_ROUTE
    i1, i2, w1, w2 = _route(lg_ref[...], rb_ref[...])

    eio = lax.broadcasted_iota(jnp.int32, (N_EXP, tm), 0)
    oh1 = eio == i1
    oh2 = eio == i2
    oh = (oh1 | oh2).astype(F32)
    nc = ROUTE_CHUNK
    tri = (lax.broadcasted_iota(jnp.int32, (nc, nc), 0) < lax.broadcasted_iota(jnp.int32, (nc, nc), 1)).astype(BF16)
    lane = lax.broadcasted_iota(jnp.int32, (N_EXP, 128), 1)
    counts = jnp.zeros((N_EXP, 128), F32)
    pos = []
    for s0 in range(0, tm, SORT_TL):
        base = jnp.zeros((N_EXP, 1), F32)
        for c0 in range(s0, s0 + SORT_TL, nc):
            ohc = oh[:, c0:c0 + nc]
            pos.append(base + jnp.dot(ohc.astype(BF16), tri, preferred_element_type=F32))
            base = base + jnp.sum(ohc, axis=1, keepdims=True)
        counts = jnp.where(lane == s0 // SORT_TL, base, counts)
    pos = jnp.concatenate(pos, axis=1)
    r1 = jnp.sum(jnp.where(oh1, pos, 0.0), axis=0, keepdims=True).astype(jnp.int32)
    r2 = jnp.sum(jnp.where(oh2, pos, 0.0), axis=0, keepdims=True).astype(jnp.int32)
    cnt_ref[0] = counts

    sub = lax.broadcasted_iota(jnp.int32, (8, tm), 0)
    zi = jnp.zeros((8, tm), jnp.int32)
    ri_ref[...] = jnp.where(sub == 0, i1, jnp.where(sub == 1, i2, jnp.where(sub == 2, r1, jnp.where(sub == 3, r2, zi))))
    rw_ref[...] = jnp.where(sub == 0, w1, jnp.where(sub == 1, w2, jnp.zeros((8, tm), F32)))


def _router(logits_t, rb):
    tm = TM_ROUTE
    lanes = lambda rows: pl.BlockSpec((rows, tm), lambda i: (0, i))
    return pl.pallas_call(
        _router_body,
        out_shape=(
            jax.ShapeDtypeStruct((8, T_ALL), jnp.int32),
            jax.ShapeDtypeStruct((8, T_ALL), F32),
            jax.ShapeDtypeStruct((T_ALL // tm, N_EXP, 128), F32),
        ),
        grid=(T_ALL // tm,),
        in_specs=[lanes(N_EXP), pl.BlockSpec((N_EXP, 1), lambda i: (0, 0))],
        out_specs=(lanes(8), lanes(8), pl.BlockSpec((1, N_EXP, 128), lambda i: (i, 0, 0))),
        compiler_params=pltpu.CompilerParams(dimension_semantics=("arbitrary",)),
        name="router",
    )(logits_t, rb)


def _sort_tables(ri, cnt):
    per = TM_ROUTE // SORT_TL
    c = jnp.transpose(cnt[:, :, :per], (0, 2, 1)).reshape(N_SORT, N_EXP).astype(jnp.int32)
    pc = (c + (CHUNK - 1)) // CHUNK * CHUNK
    loff = jnp.cumsum(pc, axis=1) - pc
    used = jnp.sum(pc, axis=1)
    seg = jnp.sum(pc, axis=0)
    ends = jnp.cumsum(seg)
    starts = ends - seg
    gstart = starts[None, :] + jnp.cumsum(pc, axis=0) - pc
    loff_t = jnp.repeat(loff.T, SORT_TL, axis=1)
    eid = jnp.arange(N_EXP, dtype=jnp.int32)[None, :, None]
    lpos = jnp.sum(jnp.where(ri[0:2, None, :] == eid, loff_t[None], 0), axis=1) + ri[2:4]
    flat = lambda a: (a // CHUNK).reshape(-1).astype(jnp.int32)
    return (lpos.astype(jnp.int32), flat(pc), flat(loff), flat(gstart), (used // CHUNK).astype(jnp.int32),
            starts.astype(jnp.int32), ends.astype(jnp.int32))


def _run_copies(nch_ref, sch_ref, dch_ref, tile, make_copy, act):
    for e in range(N_EXP):
        n = nch_ref[tile * N_EXP + e]
        s0 = sch_ref[tile * N_EXP + e]
        d0 = dch_ref[tile * N_EXP + e]

        def body(j, carry, s0=s0, d0=d0):
            act(make_copy(s0 + j, d0 + j))
            return carry

        lax.fori_loop(0, n, body, 0)


def _sort_body(nch_ref, sch_ref, dch_ref, tail_ref, lp_ref, h_ref, xs_hbm, xl_ref, sem):
    i = pl.program_id(0)
    slot = i % 2
    last = pl.num_programs(0) - 1
    r = lax.broadcasted_iota(jnp.int32, (CAP, SORT_TL), 0)
    place = ((r == lp_ref[0:1, :]) | (r == lp_ref[1:2, :])).astype(BF16)
    xl_ref[slot] = jnp.dot(place, h_ref[...], preferred_element_type=F32).astype(BF16)

    def copy_from(buf):
        def make(src, dst):
            return pltpu.make_async_copy(xl_ref.at[buf, pl.ds(src * CHUNK, CHUNK), :],
                                         xs_hbm.at[pl.ds(dst * CHUNK, CHUNK), :], sem.at[buf])
        return make

    def tail_copies(act):
        def body(j, carry):
            act(copy_from(slot)(N_CHUNK - 1, tail_ref[0] + j))
            return carry
        lax.fori_loop(0, tail_ref[1], body, 0)

    _run_copies(nch_ref, sch_ref, dch_ref, i, copy_from(slot), lambda c: c.start())

    @pl.when(i == last)
    def _():
        tail_copies(lambda c: c.start())

    @pl.when(i > 0)
    def _():
        _run_copies(nch_ref, sch_ref, dch_ref, i - 1, copy_from(1 - slot), lambda c: c.wait())

    @pl.when(i == last)
    def _():
        _run_copies(nch_ref, sch_ref, dch_ref, i, copy_from(slot), lambda c: c.wait())
        tail_copies(lambda c: c.wait())


def _sort_scatter(nch, sch, dch, tail, lpos8, h):
    return pl.pallas_call(
        _sort_body,
        out_shape=jax.ShapeDtypeStruct((XS_ROWS, D), BF16),
        grid_spec=pltpu.PrefetchScalarGridSpec(
            num_scalar_prefetch=4,
            grid=(N_SORT,),
            in_specs=[pl.BlockSpec((8, SORT_TL), lambda i, *_: (0, i)),
                      pl.BlockSpec((SORT_TL, D), lambda i, *_: (i, 0))],
            out_specs=pl.BlockSpec(memory_space=pl.ANY),
            scratch_shapes=[pltpu.VMEM((2, CAP, D), BF16), pltpu.SemaphoreType.DMA((2,))],
        ),
        compiler_params=pltpu.CompilerParams(dimension_semantics=("arbitrary",), vmem_limit_bytes=VMEM_LIMIT),
        name="sort_scatter",
    )(nch, sch, dch, tail, lpos8, h)


def _ffn_body(tile_ref, exp_ref, lo_ref, hi_ref, first_ref, mode_ref, fresh_ref,
              x_ref, wg_ref, wu_ref, wd_ref, o_ref, wg_b, wu_b, wd_b):
    u = pl.program_id(0)

    @pl.when(mode_ref[u] == 2)
    def _():
        o_ref[...] = jnp.zeros_like(o_ref)

    @pl.when(fresh_ref[u] == 1)
    def _():
        wg_b[...] = wg_ref[0, 0].astype(BF16)
        wu_b[...] = wu_ref[0, 0].astype(BF16)
        wd_b[...] = wd_ref[0, 0].astype(BF16)

    @pl.when(mode_ref[u] == 1)
    def _():
        parts = []
        for r0 in range(0, TM_FFN, FFN_SUB):
            x = x_ref[r0:r0 + FFN_SUB, :]
            g = jnp.dot(x, wg_b[...], preferred_element_type=F32)
            up = jnp.dot(x, wu_b[...], preferred_element_type=F32)
            rows = r0 + lax.broadcasted_iota(jnp.int32, (FFN_SUB, 1), 0)
            mine = (rows >= lo_ref[u]) & (rows < hi_ref[u])
            hid = jnp.where(mine, g * _sigmoid(g) * up, 0.0).astype(BF16)
            parts.append(jnp.dot(hid, wd_b[...], preferred_element_type=F32).astype(BF16))
        y = jnp.concatenate(parts, axis=0)

        @pl.when(first_ref[u] == 1)
        def _():
            o_ref[...] = y

        @pl.when(first_ref[u] == 0)
        def _():
            o_ref[...] += y


def _ffn(units, xs, wg, wu, wd, layer):
    xmap = lambda u, tile, *_: (tile[u], 0)
    wmap = lambda u, tile, exp, *_: (layer, exp[u], 0, 0)
    return pl.pallas_call(
        _ffn_body,
        out_shape=jax.ShapeDtypeStruct((XS_ROWS, D), BF16),
        grid_spec=pltpu.PrefetchScalarGridSpec(
            num_scalar_prefetch=7,
            grid=(N_UNITS,),
            in_specs=[pl.BlockSpec((TM_FFN, D), xmap),
                      pl.BlockSpec((1, 1, D, D_EXP), wmap), pl.BlockSpec((1, 1, D, D_EXP), wmap),
                      pl.BlockSpec((1, 1, D_EXP, D), wmap)],
            out_specs=pl.BlockSpec((TM_FFN, D), xmap),
            scratch_shapes=[pltpu.VMEM((D, D_EXP), BF16), pltpu.VMEM((D, D_EXP), BF16), pltpu.VMEM((D_EXP, D), BF16)],
        ),
        compiler_params=pltpu.CompilerParams(dimension_semantics=("arbitrary",), vmem_limit_bytes=VMEM_LIMIT),
        name="expert_ffn",
    )(*units, xs, wg, wu, wd)


def _ffn_units(starts, ends):
    total_rows = ends[-1]
    t0 = jnp.arange(N_FFN_TILES, dtype=jnp.int32) * TM_FFN
    t1 = jnp.minimum(t0 + TM_FFN, total_rows) - 1
    e_first = jnp.sum(ends[None, :] <= t0[:, None], axis=1).astype(jnp.int32)
    e_last = jnp.sum(ends[None, :] <= t1[:, None], axis=1).astype(jnp.int32)
    n_per = jnp.where(t0 < total_rows, e_last - e_first + 1, 0)
    u_end = jnp.cumsum(n_per)
    u_start = u_end - n_per
    total = u_end[-1]
    u = jnp.arange(N_UNITS, dtype=jnp.int32)
    uc = jnp.minimum(u, total - 1)
    tile = jnp.sum(u_end[None, :] <= uc[:, None], axis=1).astype(jnp.int32)
    exp = e_first[tile] + (uc - u_start[tile])
    lo = jnp.clip(starts[exp] - tile * TM_FFN, 0, TM_FFN)
    hi = jnp.clip(ends[exp] - tile * TM_FFN, 0, TM_FFN)
    first = ((uc == u_start[tile]) & (u < total)).astype(jnp.int32)
    exp = exp.astype(jnp.int32)
    fresh = jnp.concatenate([jnp.ones((1,), jnp.int32), (exp[1:] != exp[:-1]).astype(jnp.int32)])
    spare = jnp.sum(t0 < total_rows).astype(jnp.int32) + (u - total)
    mode = jnp.where(u < total, 1, jnp.where(spare < N_FFN_TILES, 2, 0)).astype(jnp.int32)
    tile = jnp.where(u < total, tile, jnp.minimum(spare, N_FFN_TILES - 1)).astype(jnp.int32)
    return tile, exp, lo.astype(jnp.int32), hi.astype(jnp.int32), first, mode, fresh


def _combine_body(nch_ref, sch_ref, dch_ref, used_ref, lp_ref, w_ref, x1_ref, mod_ref, ys_hbm, o_ref, yl_ref, sem):
    i = pl.program_id(0)
    slot = i % 2
    last = pl.num_programs(0) - 1

    def copy_into(buf):
        def make(loc, glob):
            return pltpu.make_async_copy(ys_hbm.at[pl.ds(glob * CHUNK, CHUNK), :],
                                         yl_ref.at[buf, pl.ds(loc * CHUNK, CHUNK), :], sem.at[buf])
        return make

    @pl.when(i == 0)
    def _():
        _run_copies(nch_ref, sch_ref, dch_ref, i, copy_into(slot), lambda c: c.start())

    @pl.when(i < last)
    def _():
        _run_copies(nch_ref, sch_ref, dch_ref, i + 1, copy_into(1 - slot), lambda c: c.start())

    row = _mod_row(i, SORT_TL)
    g2 = mod_ref[pl.ds(row, 1), 5 * D:6 * D]
    lane = lax.broadcasted_iota(jnp.int32, (SORT_TL, CAP), 1)
    comb = (jnp.where(lane == lp_ref[:, 0:1], w_ref[:, 0:1], 0.0)
            + jnp.where(lane == lp_ref[:, 1:2], w_ref[:, 1:2], 0.0)).astype(BF16)

    _run_copies(nch_ref, sch_ref, dch_ref, i, copy_into(slot), lambda c: c.wait())

    def clear(j, carry):
        yl_ref[slot, pl.ds(pl.multiple_of(j * CHUNK, CHUNK), CHUNK), :] = jnp.zeros((CHUNK, D), BF16)
        return carry

    lax.fori_loop(used_ref[i], N_CHUNK, clear, 0)
    y = jnp.dot(comb, yl_ref[slot], preferred_element_type=F32)
    o_ref[...] = x1_ref[...] + g2 * y


def _combine(nch, sch, dch, used, lpos_tok, w_tok, x1, mod_l, ys):
    tok = lambda w: pl.BlockSpec((SORT_TL, w), lambda i, *_: (i, 0))
    return pl.pallas_call(
        _combine_body,
        out_shape=jax.ShapeDtypeStruct((T_ALL, D), F32),
        grid_spec=pltpu.PrefetchScalarGridSpec(
            num_scalar_prefetch=4,
            grid=(N_SORT,),
            in_specs=[tok(2), tok(2), tok(D),
                      pl.BlockSpec((MOD_ROWS, 6 * D), lambda i, *_: (0, 0)),
                      pl.BlockSpec(memory_space=pl.ANY)],
            out_specs=tok(D),
            scratch_shapes=[pltpu.VMEM((2, CAP, D), BF16), pltpu.SemaphoreType.DMA((2,))],
        ),
        compiler_params=pltpu.CompilerParams(dimension_semantics=("arbitrary",), vmem_limit_bytes=VMEM_LIMIT),
        name="combine",
    )(nch, sch, dch, used, lpos_tok, w_tok, x1, mod_l, ys)


def _final_norm_body(x_ref, g_ref, o_ref):
    x = x_ref[...]
    ms = jnp.mean(x * x, axis=-1, keepdims=True)
    o_ref[...] = x * lax.rsqrt(ms + EPS) * g_ref[...]


def _final_norm(x, gain, row0, rows):
    tm = 512
    first = row0 // tm
    return pl.pallas_call(
        _final_norm_body,
        out_shape=jax.ShapeDtypeStruct((rows, D), F32),
        grid=(rows // tm,),
        in_specs=[pl.BlockSpec((tm, D), lambda i: (first + i, 0)), pl.BlockSpec((1, D), lambda i: (0, 0))],
        out_specs=pl.BlockSpec((tm, D), lambda i: (i, 0)),
        name="final_norm",
    )(x, gain)


def _rope_tables():
    pos = jnp.arange(LAT_L)
    rowp = (pos // GRID_W).astype(F32)
    colp = (pos % GRID_W).astype(F32)
    pairs = HD // 4
    inv = ROPE_BASE ** (-jnp.arange(pairs, dtype=F32) / pairs)
    ang = jnp.concatenate([rowp[:, None] * inv] * 2 + [colp[:, None] * inv] * 2, axis=-1)
    cos = jnp.tile(jnp.cos(ang), (1, N_HEADS))
    sin = jnp.tile(jnp.sin(ang), (1, N_HEADS))
    cos = jnp.concatenate([jnp.ones((TM_IN, ATTN_W), F32), cos], axis=0)
    sin = jnp.concatenate([jnp.zeros((TM_IN, ATTN_W), F32), sin], axis=0)
    return cos, sin


def _head_matrices():
    i = jnp.arange(ATTN_W)
    bd = (i[:, None] // HD == i[None, :] // HD).astype(BF16)
    half = HD // 4
    j = i[None, :]
    src = i[:, None]
    first = (j % (2 * half)) < half
    rm = jnp.where(first & (src == j + half), -1.0, 0.0) + jnp.where(~first & (src == j - half), 1.0, 0.0)
    return bd, rm.astype(BF16)


def kernel(x_prompt, x_sample, cache_k, cache_v, c, c_ctx, w_ada, b_ada, norm1, norm2, w_in, q_norm, k_norm,
           sink, w_attn_proj, dw_w, dw_b, cln_g, cln_b, w_conv_proj, w_out, w_router, router_bias,
           w_e_gate, w_e_up, w_e_down, final_norm):
    x = jnp.concatenate([x_prompt.reshape(T_CTX, D), x_sample.reshape(T_LAT, D)], axis=0)
    cond = jnp.concatenate([c_ctx[None, :], c, jnp.zeros((MOD_ROWS - 1 - LAT_B, D), F32)], axis=0)
    mod = _ada_table(cond, w_ada, b_ada)
    cos_t, sin_t = _rope_tables()
    bd, rm = _head_matrices()
    w_in_b = w_in.astype(BF16)
    wap_b = w_attn_proj.astype(BF16)
    wcp_b = w_conv_proj.astype(BF16)
    wo_b = w_out.astype(BF16)
    wr_t = w_router.T
    rb = router_bias.reshape(N_EXP, 1)
    ck = cache_k.reshape(LAT_B, DEPTH, PAST, KV_W)
    cv = cache_v.reshape(LAT_B, DEPTH, PAST, KV_W)

    new_k, new_v = [], []
    for l in range(DEPTH):
        q, k, v, u, gates = _inproj(x, mod[l], norm1[l][None, :], w_in_b[l],
                                    jnp.tile(q_norm[l], N_HEADS)[None, :], jnp.tile(k_norm[l], N_KV)[None, :],
                                    cos_t, sin_t, bd, rm)
        new_k.append(k[:T_CTX].reshape(CTX_B, CTX_L, N_KV, HD))
        new_v.append(v[:T_CTX].reshape(CTX_B, CTX_L, N_KV, HD))
        o_ctx = _attn_ctx(sink[l], q, k, v)
        o_lat = _attn_lat(sink[l], q, k, v, ck, cv, l)
        conv_o = _conv(u, dw_w[l], dw_b[l][None, :], cln_g[l][None, :], cln_b[l][None, :])
        x1, h, logits_t = _merge(o_ctx, o_lat, conv_o, gates, x, mod[l], norm2[l][None, :],
                                 wap_b[l], wcp_b[l], wo_b[l], wr_t)
        ri, rw, cnt = _router(logits_t, rb)
        lpos, nch, sch, dch, used, starts, ends = _sort_tables(ri, cnt)
        rows_ch = ends[-1] // CHUNK
        tail = jnp.stack([rows_ch, XS_ROWS // CHUNK - rows_ch]).astype(jnp.int32)
        lpos8 = jnp.concatenate([lpos, jnp.zeros((6, T_ALL), jnp.int32)], axis=0)
        xs = _sort_scatter(nch, sch, dch, tail, lpos8, h)
        ys = _ffn(_ffn_units(starts, ends), xs, w_e_gate, w_e_up, w_e_down, l)
        x = _combine(nch, sch, dch, used, lpos.T, rw[0:2].T, x1, mod[l], ys)

    gain = final_norm[None, :]
    y_prompt = _final_norm(x, gain, 0, T_CTX).reshape(CTX_B, CTX_L, D)
    y_sample = _final_norm(x, gain, T_CTX, T_LAT).reshape(LAT_B, LAT_L, D)
    return y_prompt, y_sample, jnp.stack(new_k, axis=1), jnp.stack(new_v, axis=1)
```

```python
import jax
import jax.numpy as jnp
from jax import lax
from jax.experimental import pallas as pl
from jax.experimental.pallas import tpu as pltpu

F32 = jnp.float32
BF16 = jnp.bfloat16

D = 1024
DEPTH = 4
CTX_B, CTX_L = 32, 256
LAT_B, LAT_L = 8, 1024
PAST = 512
T_CTX = CTX_B * CTX_L
T_LAT = LAT_B * LAT_L
T_ALL = T_CTX + T_LAT
GRID_W = 64
HD = 64
N_HEADS = 8
N_KV = 2
ATTN_W = N_HEADS * HD
KV_W = N_KV * HD
WINDOW = 128
LAT_QB = 256
CONV_W = 512
CONV_K = 31
CONV_PAD = CONV_K // 2
N_EXP = 16
N_GROUPS = 4
EXP_PER_GROUP = N_EXP // N_GROUPS
D_EXP = 512
IN_W = ATTN_W + 2 * KV_W + 2 * CONV_W + 2 * D
OFF_K = ATTN_W
OFF_V = ATTN_W + KV_W
OFF_A = ATTN_W + 2 * KV_W
OFF_B = OFF_A + CONV_W
OFF_G = OFF_A + 2 * CONV_W
EPS = 1e-6
NEG = -1e30
ROPE_BASE = 10000.0
MOD_ROWS = 16

TM_IN = 512
TM_MERGE = 512
TM_ROUTE = 2048
ROUTE_CHUNK = 256
TM_CONV = 256
CONV_HALO = 16
SORT_TL = 512
N_SORT = T_ALL // SORT_TL
CHUNK = 16
CAP = 2 * SORT_TL + 2 * 128
N_CHUNK = CAP // CHUNK
TM_FFN = 512
FFN_SUB = 256
XS_ROWS = 2 * T_ALL + N_SORT * N_EXP * (CHUNK - 1)
N_FFN_TILES = XS_ROWS // TM_FFN
N_UNITS = N_FFN_TILES + N_EXP - 1

VMEM_LIMIT = 56 * 1024 * 1024


def _sigmoid(x):
    return 1.0 / (1.0 + jnp.exp(-x))


def _mod_row(tile, tm):
    start = tile * tm
    return jnp.where(start < T_CTX, 0, 1 + (start - T_CTX) // LAT_L)


def _ada_body(cond_ref, w_ref, b_ref, o_ref):
    c = cond_ref[...]
    s = (c * _sigmoid(c)).astype(BF16)
    o_ref[0] = jnp.dot(s, w_ref[0].astype(BF16), preferred_element_type=F32) + b_ref[0]


def _ada_table(cond, w_ada, b_ada):
    nj = 6 * D // 1024
    return pl.pallas_call(
        _ada_body,
        out_shape=jax.ShapeDtypeStruct((DEPTH, MOD_ROWS, 6 * D), F32),
        grid=(DEPTH, nj),
        in_specs=[
            pl.BlockSpec((MOD_ROWS, D), lambda l, j: (0, 0)),
            pl.BlockSpec((1, D, 1024), lambda l, j: (l, 0, j)),
            pl.BlockSpec((1, 1, 1024), lambda l, j: (l, 0, j)),
        ],
        out_specs=pl.BlockSpec((1, MOD_ROWS, 1024), lambda l, j: (l, 0, j)),
        name="ada_table",
    )(cond, w_ada, b_ada.reshape(DEPTH, 1, 6 * D))


def _head_norm_rope(y, gain, cos, sin, bd, rm):
    ss = jnp.dot((y * y).astype(BF16), bd, preferred_element_type=F32)
    yn = y * lax.rsqrt(ss * (1.0 / HD) + EPS) * gain
    rot = jnp.dot(yn.astype(BF16), rm, preferred_element_type=F32)
    return yn * cos + rot * sin


def _inproj_body(x_ref, mod_ref, n1_ref, w_ref, qg_ref, kg_ref, cos_ref, sin_ref, bd_ref, rm_ref,
                 q_ref, k_ref, v_ref, u_ref, g_ref):
    row = _mod_row(pl.program_id(0), TM_IN)
    sh = mod_ref[pl.ds(row, 1), 0:D]
    sc = mod_ref[pl.ds(row, 1), D:2 * D]
    x = x_ref[...]
    ms = jnp.mean(x * x, axis=-1, keepdims=True)
    h = ((x * lax.rsqrt(ms + EPS) * n1_ref[...]) * (1.0 + sc) + sh).astype(BF16)

    def proj(lo, hi):
        return jnp.dot(h, w_ref[:, lo:hi], preferred_element_type=F32)

    cos = cos_ref[...]
    sin = sin_ref[...]
    q = _head_norm_rope(proj(0, OFF_K), qg_ref[...], cos, sin, bd_ref[...], rm_ref[...])
    q_ref[...] = (q * (HD ** -0.5)).astype(BF16)
    k = _head_norm_rope(proj(OFF_K, OFF_V), kg_ref[...], cos[:, :KV_W], sin[:, :KV_W],
                        bd_ref[:KV_W, :KV_W], rm_ref[:KV_W, :KV_W])
    k_ref[...] = k
    v_ref[...] = proj(OFF_V, OFF_A)
    a = proj(OFF_A, OFF_B)
    b = proj(OFF_B, OFF_G)
    u_ref[...] = (a * _sigmoid(b)).astype(BF16)
    for j in range(2):
        g = proj(OFF_G + j * D, OFF_G + (j + 1) * D)
        g_ref[:, j * D:(j + 1) * D] = _sigmoid(g).astype(BF16)


def _inproj(x, mod_l, n1, w_in, qg, kg, cos_t, sin_t, bd, rm):
    n_ctx_tiles = T_CTX // TM_IN
    tiles_per_seq = LAT_L // TM_IN

    def tab(i):
        return (jnp.where(i < n_ctx_tiles, 0, 1 + (i - n_ctx_tiles) % tiles_per_seq), 0)

    full = lambda shape: pl.BlockSpec(shape, lambda i: (0,) * len(shape))
    row = lambda w: pl.BlockSpec((TM_IN, w), lambda i: (i, 0))
    return pl.pallas_call(
        _inproj_body,
        out_shape=(
            jax.ShapeDtypeStruct((T_ALL, ATTN_W), BF16),
            jax.ShapeDtypeStruct((T_ALL, KV_W), F32),
            jax.ShapeDtypeStruct((T_ALL, KV_W), F32),
            jax.ShapeDtypeStruct((T_ALL, CONV_W), BF16),
            jax.ShapeDtypeStruct((T_ALL, 2 * D), BF16),
        ),
        grid=(T_ALL // TM_IN,),
        in_specs=[
            row(D), full((MOD_ROWS, 6 * D)), full((1, D)), full((D, IN_W)),
            full((1, ATTN_W)), full((1, KV_W)),
            pl.BlockSpec((TM_IN, ATTN_W), tab), pl.BlockSpec((TM_IN, ATTN_W), tab),
            full((ATTN_W, ATTN_W)), full((ATTN_W, ATTN_W)),
        ],
        out_specs=(row(ATTN_W), row(KV_W), row(KV_W), row(CONV_W), row(2 * D)),
        compiler_params=pltpu.CompilerParams(dimension_semantics=("arbitrary",), vmem_limit_bytes=VMEM_LIMIT),
        name="inproj",
    )(x, mod_l, n1, w_in, qg, kg, cos_t, sin_t, bd, rm)


def _head_pair_kv(k, v):
    lane = lax.broadcasted_iota(jnp.int32, k.shape, 1)
    low = lane < HD
    zero = jnp.zeros_like(k)
    k_sw = pltpu.roll(k, HD, 1)
    v_sw = pltpu.roll(v, HD, 1)
    g0 = (jnp.where(low, k, zero), jnp.where(low, v, zero), jnp.where(low, zero, k_sw), jnp.where(low, zero, v_sw))
    g1 = (jnp.where(low, k_sw, zero), jnp.where(low, v_sw, zero), jnp.where(low, zero, k), jnp.where(low, zero, v))
    return tuple(tuple(t.astype(BF16) for t in g) for g in (g0, g1))


def _sink_attend(qp, kk, vv, sink, mask):
    s = lax.dot_general(qp, kk, (((1,), (1,)), ((), ())), preferred_element_type=F32)
    if mask is not None:
        s = jnp.where(mask, s, NEG)
    m = jnp.maximum(jnp.max(s, axis=-1, keepdims=True), sink)
    p = jnp.exp(s - m)
    den = jnp.sum(p, axis=-1, keepdims=True) + jnp.exp(sink - m)
    return jnp.dot(p.astype(BF16), vv, preferred_element_type=F32) / den


def _attend_per_pair(sink_ref, q_ref, o_ref, kv):
    for pair in range(N_HEADS // 2):
        k_lo, v_lo, k_hi, v_hi = kv[pair // 2]
        qp = q_ref[:, pair * 128:(pair + 1) * 128]
        o = (_sink_attend(qp, k_lo, v_lo, sink_ref[2 * pair], None)
             + _sink_attend(qp, k_hi, v_hi, sink_ref[2 * pair + 1], None))
        o_ref[:, pair * 128:(pair + 1) * 128] = o.astype(BF16)


def _attend_stacked(sink_ref, q_ref, o_ref, kv, mask):
    nq = q_ref.shape[0]
    mask = jnp.concatenate([mask, mask], axis=0)
    for g in range(N_KV):
        k_lo, v_lo, k_hi, v_hi = kv[g]
        q2 = jnp.concatenate([q_ref[:, (2 * g) * 128:(2 * g + 1) * 128],
                              q_ref[:, (2 * g + 1) * 128:(2 * g + 2) * 128]], axis=0)

        def sinks(parity):
            top = jnp.full((nq, 1), sink_ref[4 * g + parity], F32)
            bot = jnp.full((nq, 1), sink_ref[4 * g + 2 + parity], F32)
            return jnp.concatenate([top, bot], axis=0)

        o = _sink_attend(q2, k_lo, v_lo, sinks(0), mask) + _sink_attend(q2, k_hi, v_hi, sinks(1), mask)
        o_ref[:, (2 * g) * 128:(2 * g + 1) * 128] = o[:nq].astype(BF16)
        o_ref[:, (2 * g + 1) * 128:(2 * g + 2) * 128] = o[nq:].astype(BF16)


def _attn_ctx_body(sink_ref, q_ref, k_ref, v_ref, o_ref):
    _attend_per_pair(sink_ref, q_ref, o_ref, _head_pair_kv(k_ref[...], v_ref[...]))


def _attn_ctx(sink_l, q, k, v):
    blk = lambda w: pl.BlockSpec((CTX_L, w), lambda b: (b, 0))
    return pl.pallas_call(
        _attn_ctx_body,
        out_shape=jax.ShapeDtypeStruct((T_CTX, ATTN_W), BF16),
        grid=(CTX_B,),
        in_specs=[pl.BlockSpec(memory_space=pltpu.SMEM), blk(ATTN_W), blk(KV_W), blk(KV_W)],
        out_specs=blk(ATTN_W),
        compiler_params=pltpu.CompilerParams(dimension_semantics=("arbitrary",)),
        name="attn_ctx",
    )(sink_l, q, k, v)


def _attn_lat_body(sink_ref, q_ref, kp_ref, kc_ref, kn_ref, vp_ref, vc_ref, vn_ref, ck_ref, cv_ref, o_ref):
    i = pl.program_id(1)
    nblk = LAT_L // LAT_QB
    k = jnp.concatenate([kp_ref[...], kc_ref[...], kn_ref[...], ck_ref[0, 0]], axis=0)
    v = jnp.concatenate([vp_ref[...], vc_ref[...], vn_ref[...], cv_ref[0, 0]], axis=0)
    nloc = LAT_QB + 2 * WINDOW
    nk = nloc + PAST
    r = lax.broadcasted_iota(jnp.int32, (LAT_QB, nk), 0)
    c = lax.broadcasted_iota(jnp.int32, (LAT_QB, nk), 1)
    local = (c - r >= 0) & (c - r <= 2 * WINDOW)
    local = local & ((c >= WINDOW) | (i > 0)) & ((c < WINDOW + LAT_QB) | (i < nblk - 1))
    mask = local | (c >= nloc)
    _attend_stacked(sink_ref, q_ref, o_ref, _head_pair_kv(k, v), mask)


def _attn_lat(sink_l, q, k, v, cache_k, cache_v, layer):
    nblk = LAT_L // LAT_QB
    per = LAT_QB // WINDOW
    nhalo = LAT_L // WINDOW
    base = T_CTX // LAT_QB
    hbase = T_CTX // WINDOW
    cur = lambda b, i: (base + b * nblk + i, 0)
    prev = lambda b, i: (hbase + b * nhalo + jnp.maximum(i * per - 1, 0), 0)
    nxt = lambda b, i: (hbase + b * nhalo + jnp.minimum((i + 1) * per, nhalo - 1), 0)
    kvb = lambda rows, f: pl.BlockSpec((rows, KV_W), f)
    cache = pl.BlockSpec((1, 1, PAST, KV_W), lambda b, i: (b, layer, 0, 0))
    return pl.pallas_call(
        _attn_lat_body,
        out_shape=jax.ShapeDtypeStruct((T_LAT, ATTN_W), BF16),
        grid=(LAT_B, nblk),
        in_specs=[pl.BlockSpec(memory_space=pltpu.SMEM), pl.BlockSpec((LAT_QB, ATTN_W), cur),
                  kvb(WINDOW, prev), kvb(LAT_QB, cur), kvb(WINDOW, nxt),
                  kvb(WINDOW, prev), kvb(LAT_QB, cur), kvb(WINDOW, nxt), cache, cache],
        out_specs=pl.BlockSpec((LAT_QB, ATTN_W), lambda b, i: (b * nblk + i, 0)),
        compiler_params=pltpu.CompilerParams(dimension_semantics=("arbitrary", "arbitrary")),
        name="attn_lat",
    )(sink_l, q, k, k, k, v, v, v, cache_k, cache_v)


def _conv_body(up_ref, uc_ref, un_ref, w_ref, b_ref, g_ref, beta_ref, o_ref, pad_ref):
    i = pl.program_id(0)
    n_ctx = T_CTX // TM_CONV
    per_seq = LAT_L // TM_CONV
    j = (i - n_ctx) % per_seq
    has_prev = (i >= n_ctx) & (j > 0)
    has_next = (i >= n_ctx) & (j < per_seq - 1)
    zero = jnp.zeros((CONV_HALO, CONV_W), F32)
    pad_ref[0, 0:CONV_HALO, :] = jnp.where(has_prev, up_ref[...].astype(F32), zero)
    pad_ref[0, CONV_HALO:CONV_HALO + TM_CONV, :] = uc_ref[...].astype(F32)
    pad_ref[0, CONV_HALO + TM_CONV:, :] = jnp.where(has_next, un_ref[...].astype(F32), zero)
    n_sh = TM_CONV + 2 * CONV_HALO - 8
    for s in range(1, 8):
        pad_ref[s, 0:n_sh, :] = pad_ref[0, s:s + n_sh, :]
    chunk = 64
    for c0 in range(0, TM_CONV, chunk):
        acc = jnp.zeros((chunk, CONV_W), F32) + b_ref[...]
        for t in range(CONV_K):
            src = CONV_HALO - CONV_PAD + t
            row = src - src % 8 + c0
            acc = acc + pad_ref[src % 8, row:row + chunk, :] * w_ref[t:t + 1, :]
        mu = jnp.mean(acc, axis=-1, keepdims=True)
        cen = acc - mu
        var = jnp.mean(cen * cen, axis=-1, keepdims=True)
        y = cen * lax.rsqrt(var + EPS) * g_ref[...] + beta_ref[...]
        o_ref[c0:c0 + chunk, :] = (y * _sigmoid(y)).astype(BF16)


def _conv(u, dw_w, dw_b, ln_g, ln_b):
    per = TM_CONV // CONV_HALO
    last = T_ALL // CONV_HALO - 1
    full = lambda shape: pl.BlockSpec(shape, lambda i: (0,) * len(shape))
    return pl.pallas_call(
        _conv_body,
        out_shape=jax.ShapeDtypeStruct((T_ALL, CONV_W), BF16),
        grid=(T_ALL // TM_CONV,),
        in_specs=[
            pl.BlockSpec((CONV_HALO, CONV_W), lambda i: (jnp.maximum(i * per - 1, 0), 0)),
            pl.BlockSpec((TM_CONV, CONV_W), lambda i: (i, 0)),
            pl.BlockSpec((CONV_HALO, CONV_W), lambda i: (jnp.minimum((i + 1) * per, last), 0)),
            full((CONV_K, CONV_W)), full((1, CONV_W)), full((1, CONV_W)), full((1, CONV_W)),
        ],
        out_specs=pl.BlockSpec((TM_CONV, CONV_W), lambda i: (i, 0)),
        scratch_shapes=[pltpu.VMEM((8, TM_CONV + 2 * CONV_HALO, CONV_W), F32)],
        compiler_params=pltpu.CompilerParams(dimension_semantics=("arbitrary",)),
        name="conv_module",
    )(u, u, u, dw_w, dw_b, ln_g, ln_b)


def _row_pick(idx, rows):
    out = jnp.zeros_like(rows[0])
    for e, r in enumerate(rows):
        out = jnp.where(idx == e, r, out)
    return out


def _argmax_rows(rows):
    best = rows[0]
    idx = jnp.zeros(rows[0].shape, jnp.int32)
    for e in range(1, len(rows)):
        upd = rows[e] > best
        idx = jnp.where(upd, e, idx)
        best = jnp.where(upd, rows[e], best)
    return idx


def _route(logits_t, bias):
    scores = _sigmoid(logits_t)
    sel = scores + bias
    s_rows = [scores[e:e + 1, :] for e in range(N_EXP)]
    rows = [sel[e:e + 1, :] for e in range(N_EXP)]
    grp = []
    for g in range(N_GROUPS):
        r = rows[g * EXP_PER_GROUP:(g + 1) * EXP_PER_GROUP]
        best = None
        for a in range(EXP_PER_GROUP):
            for b in range(a + 1, EXP_PER_GROUP):
                s = r[a] + r[b]
                best = s if best is None else jnp.maximum(best, s)
        grp.append(best)
    gidx = _argmax_rows(grp)
    masked = [jnp.where(gidx == e // EXP_PER_GROUP, rows[e], NEG) for e in range(N_EXP)]
    i1 = _argmax_rows(masked)
    i2 = _argmax_rows([jnp.where(i1 == e, -jnp.inf, masked[e]) for e in range(N_EXP)])
    w1 = _row_pick(i1, s_rows)
    w2 = _row_pick(i2, s_rows)
    tot = w1 + w2
    return i1, i2, w1 / tot, w2 / tot


def _merge_body(ac_ref, al_ref, c_ref, g_ref, x_ref, mod_ref, n2_ref, wap_ref, wcp_ref, wo_ref, wr_ref,
                x1_ref, h_ref, lg_ref):
    step = pl.program_id(0)
    row = _mod_row(step, TM_MERGE)
    g1 = mod_ref[pl.ds(row, 1), 2 * D:3 * D]
    sh2 = mod_ref[pl.ds(row, 1), 3 * D:4 * D]
    sc2 = mod_ref[pl.ds(row, 1), 4 * D:5 * D]
    a = jnp.where(step < T_CTX // TM_MERGE, ac_ref[...], al_ref[...])
    attn = jnp.dot(a, wap_ref[...], preferred_element_type=F32)
    conv = jnp.dot(c_ref[...], wcp_ref[...], preferred_element_type=F32)
    merged = g_ref[:, 0:D].astype(F32) * attn + g_ref[:, D:2 * D].astype(F32) * conv
    mix = jnp.dot(merged.astype(BF16), wo_ref[...], preferred_element_type=F32)
    x1 = x_ref[...] + g1 * mix
    x1_ref[...] = x1
    ms = jnp.mean(x1 * x1, axis=-1, keepdims=True)
    h = (x1 * lax.rsqrt(ms + EPS) * n2_ref[...]) * (1.0 + sc2) + sh2
    h_ref[...] = h.astype(BF16)
    lg_ref[...] = lax.dot_general(wr_ref[...], h, (((1,), (1,)), ((), ())),
                                  precision=lax.Precision.HIGHEST, preferred_element_type=F32)


def _merge(attn_ctx, attn_lat, conv_o, gates, x, mod_l, n2, wap, wcp, wo, wr_t):
    tm = TM_MERGE
    n_ctx = T_CTX // tm
    full = lambda shape: pl.BlockSpec(shape, lambda i: (0,) * len(shape))
    row = lambda w: pl.BlockSpec((tm, w), lambda i: (i, 0))
    a_ctx = pl.BlockSpec((tm, ATTN_W), lambda i: (jnp.minimum(i, n_ctx - 1), 0))
    a_lat = pl.BlockSpec((tm, ATTN_W), lambda i: (jnp.maximum(i - n_ctx, 0), 0))
    return pl.pallas_call(
        _merge_body,
        out_shape=(
            jax.ShapeDtypeStruct((T_ALL, D), F32),
            jax.ShapeDtypeStruct((T_ALL, D), BF16),
            jax.ShapeDtypeStruct((N_EXP, T_ALL), F32),
        ),
        grid=(T_ALL // tm,),
        in_specs=[a_ctx, a_lat, row(CONV_W), row(2 * D), row(D), full((MOD_ROWS, 6 * D)), full((1, D)),
                  full((ATTN_W, D)), full((CONV_W, D)), full((D, D)), full((N_EXP, D))],
        out_specs=(row(D), row(D), pl.BlockSpec((N_EXP, tm), lambda i: (0, i))),
        compiler_params=pltpu.CompilerParams(dimension_semantics=("arbitrary",), vmem_limit_bytes=VMEM_LIMIT),
        name="merge",
    )(attn_ctx, attn_lat, conv_o, gates, x, mod_l, n2, wap, wcp, wo, wr_t)


def _router_body(lg_ref, rb_ref, ri_ref, rw_ref, cnt_ref):
    tm = TM_ROUTE
    i1, i2, w1, w2 = _route(lg_ref[...], rb_ref[...])

    eio = lax.broadcasted_iota(jnp.int32, (N_EXP, tm), 0)
    oh1 = eio == i1
    oh2 = eio == i2
    oh = (oh1 | oh2).astype(F32)
    nc = ROUTE_CHUNK
    tri = (lax.broadcasted_iota(jnp.int32, (nc, nc), 0) < lax.broadcasted_iota(jnp.int32, (nc, nc), 1)).astype(BF16)
    lane = lax.broadcasted_iota(jnp.int32, (N_EXP, 128), 1)
    counts = jnp.zeros((N_EXP, 128), F32)
    pos = []
    for s0 in range(0, tm, SORT_TL):
        base = jnp.zeros((N_EXP, 1), F32)
        for c0 in range(s0, s0 + SORT_TL, nc):
            ohc = oh[:, c0:c0 + nc]
            pos.append(base + jnp.dot(ohc.astype(BF16), tri, preferred_element_type=F32))
            base = base + jnp.sum(ohc, axis=1, keepdims=True)
        counts = jnp.where(lane == s0 // SORT_TL, base, counts)
    pos = jnp.concatenate(pos, axis=1)
    r1 = jnp.sum(jnp.where(oh1, pos, 0.0), axis=0, keepdims=True).astype(jnp.int32)
    r2 = jnp.sum(jnp.where(oh2, pos, 0.0), axis=0, keepdims=True).astype(jnp.int32)
    cnt_ref[0] = counts

    sub = lax.broadcasted_iota(jnp.int32, (8, tm), 0)
    zi = jnp.zeros((8, tm), jnp.int32)
    ri_ref[...] = jnp.where(sub == 0, i1, jnp.where(sub == 1, i2, jnp.where(sub == 2, r1, jnp.where(sub == 3, r2, zi))))
    rw_ref[...] = jnp.where(sub == 0, w1, jnp.where(sub == 1, w2, jnp.zeros((8, tm), F32)))


def _router(logits_t, rb):
    tm = TM_ROUTE
    lanes = lambda rows: pl.BlockSpec((rows, tm), lambda i: (0, i))
    return pl.pallas_call(
        _router_body,
        out_shape=(
            jax.ShapeDtypeStruct((8, T_ALL), jnp.int32),
            jax.ShapeDtypeStruct((8, T_ALL), F32),
            jax.ShapeDtypeStruct((T_ALL // tm, N_EXP, 128), F32),
        ),
        grid=(T_ALL // tm,),
        in_specs=[lanes(N_EXP), pl.BlockSpec((N_EXP, 1), lambda i: (0, 0))],
        out_specs=(lanes(8), lanes(8), pl.BlockSpec((1, N_EXP, 128), lambda i: (i, 0, 0))),
        compiler_params=pltpu.CompilerParams(dimension_semantics=("arbitrary",)),
        name="router",
    )(logits_t, rb)


def _sort_tables(ri, cnt):
    per = TM_ROUTE // SORT_TL
    c = jnp.transpose(cnt[:, :, :per], (0, 2, 1)).reshape(N_SORT, N_EXP).astype(jnp.int32)
    pc = (c + (CHUNK - 1)) // CHUNK * CHUNK
    loff = jnp.cumsum(pc, axis=1) - pc
    used = jnp.sum(pc, axis=1)
    seg = jnp.sum(pc, axis=0)
    ends = jnp.cumsum(seg)
    starts = ends - seg
    gstart = starts[None, :] + jnp.cumsum(pc, axis=0) - pc
    loff_t = jnp.repeat(loff.T, SORT_TL, axis=1)
    eid = jnp.arange(N_EXP, dtype=jnp.int32)[None, :, None]
    lpos = jnp.sum(jnp.where(ri[0:2, None, :] == eid, loff_t[None], 0), axis=1) + ri[2:4]
    j = jnp.arange(N_CHUNK, dtype=jnp.int32)[None, :, None]
    lo = (loff // CHUNK)[:, None, :]
    in_run = (j >= lo) & (j < lo + (pc // CHUNK)[:, None, :])
    dst = jnp.sum(jnp.where(in_run, (gstart // CHUNK)[:, None, :] + j - lo, 0), axis=-1)
    return (lpos.astype(jnp.int32), dst.reshape(-1).astype(jnp.int32), (used // CHUNK).astype(jnp.int32),
            starts.astype(jnp.int32), ends.astype(jnp.int32))


def _run_copies(dst_ref, used_ref, tile, make_copy, act):
    def body(j, carry):
        act(make_copy(j, dst_ref[tile * N_CHUNK + j]))
        return carry

    lax.fori_loop(0, used_ref[tile], body, 0)


def _sort_body(dst_ref, used_ref, tail_ref, lp_ref, h_ref, xs_hbm, xl_ref, sem):
    i = pl.program_id(0)
    slot = i % 2
    last = pl.num_programs(0) - 1
    r = lax.broadcasted_iota(jnp.int32, (CAP, SORT_TL), 0)
    place = ((r == lp_ref[0:1, :]) | (r == lp_ref[1:2, :])).astype(BF16)
    xl_ref[slot] = jnp.dot(place, h_ref[...], preferred_element_type=F32).astype(BF16)

    def copy_from(buf):
        def make(src, dst):
            return pltpu.make_async_copy(xl_ref.at[buf, pl.ds(src * CHUNK, CHUNK), :],
                                         xs_hbm.at[pl.ds(dst * CHUNK, CHUNK), :], sem.at[buf])
        return make

    def tail_copies(act):
        def body(j, carry):
            act(copy_from(slot)(N_CHUNK - 1, tail_ref[0] + j))
            return carry
        lax.fori_loop(0, tail_ref[1], body, 0)

    _run_copies(dst_ref, used_ref,i, copy_from(slot), lambda c: c.start())

    @pl.when(i == last)
    def _():
        tail_copies(lambda c: c.start())

    @pl.when(i > 0)
    def _():
        _run_copies(dst_ref, used_ref,i - 1, copy_from(1 - slot), lambda c: c.wait())

    @pl.when(i == last)
    def _():
        _run_copies(dst_ref, used_ref,i, copy_from(slot), lambda c: c.wait())
        tail_copies(lambda c: c.wait())


def _sort_scatter(dst, used, tail, lpos8, h):
    return pl.pallas_call(
        _sort_body,
        out_shape=jax.ShapeDtypeStruct((XS_ROWS, D), BF16),
        grid_spec=pltpu.PrefetchScalarGridSpec(
            num_scalar_prefetch=3,
            grid=(N_SORT,),
            in_specs=[pl.BlockSpec((8, SORT_TL), lambda i, *_: (0, i)),
                      pl.BlockSpec((SORT_TL, D), lambda i, *_: (i, 0))],
            out_specs=pl.BlockSpec(memory_space=pl.ANY),
            scratch_shapes=[pltpu.VMEM((2, CAP, D), BF16), pltpu.SemaphoreType.DMA((2,))],
        ),
        compiler_params=pltpu.CompilerParams(dimension_semantics=("arbitrary",), vmem_limit_bytes=VMEM_LIMIT),
        name="sort_scatter",
    )(dst, used, tail, lpos8, h)


def _ffn_body(tile_ref, exp_ref, lo_ref, hi_ref, first_ref, mode_ref, fresh_ref,
              x_ref, wg_ref, wu_ref, wd_ref, o_ref, wg_b, wu_b, wd_b):
    u = pl.program_id(0)

    @pl.when(mode_ref[u] == 2)
    def _():
        o_ref[...] = jnp.zeros_like(o_ref)

    @pl.when(fresh_ref[u] == 1)
    def _():
        wg_b[...] = wg_ref[0, 0].astype(BF16)
        wu_b[...] = wu_ref[0, 0].astype(BF16)
        wd_b[...] = wd_ref[0, 0].astype(BF16)

    @pl.when(mode_ref[u] == 1)
    def _():
        parts = []
        for r0 in range(0, TM_FFN, FFN_SUB):
            x = x_ref[r0:r0 + FFN_SUB, :]
            g = jnp.dot(x, wg_b[...], preferred_element_type=F32)
            up = jnp.dot(x, wu_b[...], preferred_element_type=F32)
            rows = r0 + lax.broadcasted_iota(jnp.int32, (FFN_SUB, 1), 0)
            mine = (rows >= lo_ref[u]) & (rows < hi_ref[u])
            hid = jnp.where(mine, g * _sigmoid(g) * up, 0.0).astype(BF16)
            parts.append(jnp.dot(hid, wd_b[...], preferred_element_type=F32).astype(BF16))
        y = jnp.concatenate(parts, axis=0)

        @pl.when(first_ref[u] == 1)
        def _():
            o_ref[...] = y

        @pl.when(first_ref[u] == 0)
        def _():
            o_ref[...] += y


def _ffn(units, xs, wg, wu, wd, layer):
    xmap = lambda u, tile, *_: (tile[u], 0)
    wmap = lambda u, tile, exp, *_: (layer, exp[u], 0, 0)
    return pl.pallas_call(
        _ffn_body,
        out_shape=jax.ShapeDtypeStruct((XS_ROWS, D), BF16),
        grid_spec=pltpu.PrefetchScalarGridSpec(
            num_scalar_prefetch=7,
            grid=(N_UNITS,),
            in_specs=[pl.BlockSpec((TM_FFN, D), xmap),
                      pl.BlockSpec((1, 1, D, D_EXP), wmap), pl.BlockSpec((1, 1, D, D_EXP), wmap),
                      pl.BlockSpec((1, 1, D_EXP, D), wmap)],
            out_specs=pl.BlockSpec((TM_FFN, D), xmap),
            scratch_shapes=[pltpu.VMEM((D, D_EXP), BF16), pltpu.VMEM((D, D_EXP), BF16), pltpu.VMEM((D_EXP, D), BF16)],
        ),
        compiler_params=pltpu.CompilerParams(dimension_semantics=("arbitrary",), vmem_limit_bytes=VMEM_LIMIT),
        name="expert_ffn",
    )(*units, xs, wg, wu, wd)


def _ffn_units(starts, ends):
    total_rows = ends[-1]
    t0 = jnp.arange(N_FFN_TILES, dtype=jnp.int32) * TM_FFN
    t1 = jnp.minimum(t0 + TM_FFN, total_rows) - 1
    e_first = jnp.sum(ends[None, :] <= t0[:, None], axis=1).astype(jnp.int32)
    e_last = jnp.sum(ends[None, :] <= t1[:, None], axis=1).astype(jnp.int32)
    n_per = jnp.where(t0 < total_rows, e_last - e_first + 1, 0)
    u_end = jnp.cumsum(n_per)
    u_start = u_end - n_per
    total = u_end[-1]
    u = jnp.arange(N_UNITS, dtype=jnp.int32)
    uc = jnp.minimum(u, total - 1)
    tile = jnp.sum(u_end[None, :] <= uc[:, None], axis=1).astype(jnp.int32)
    exp = e_first[tile] + (uc - u_start[tile])
    lo = jnp.clip(starts[exp] - tile * TM_FFN, 0, TM_FFN)
    hi = jnp.clip(ends[exp] - tile * TM_FFN, 0, TM_FFN)
    first = ((uc == u_start[tile]) & (u < total)).astype(jnp.int32)
    exp = exp.astype(jnp.int32)
    fresh = jnp.concatenate([jnp.ones((1,), jnp.int32), (exp[1:] != exp[:-1]).astype(jnp.int32)])
    spare = jnp.sum(t0 < total_rows).astype(jnp.int32) + (u - total)
    mode = jnp.where(u < total, 1, jnp.where(spare < N_FFN_TILES, 2, 0)).astype(jnp.int32)
    tile = jnp.where(u < total, tile, jnp.minimum(spare, N_FFN_TILES - 1)).astype(jnp.int32)
    return tile, exp, lo.astype(jnp.int32), hi.astype(jnp.int32), first, mode, fresh


def _combine_body(dst_ref, used_ref, lp_ref, w_ref, x1_ref, mod_ref, ys_hbm, o_ref, yl_ref, sem):
    i = pl.program_id(0)
    slot = i % 2
    last = pl.num_programs(0) - 1

    def copy_into(buf):
        def make(loc, glob):
            return pltpu.make_async_copy(ys_hbm.at[pl.ds(glob * CHUNK, CHUNK), :],
                                         yl_ref.at[buf, pl.ds(loc * CHUNK, CHUNK), :], sem.at[buf])
        return make

    @pl.when(i == 0)
    def _():
        _run_copies(dst_ref, used_ref,i, copy_into(slot), lambda c: c.start())

    @pl.when(i < last)
    def _():
        _run_copies(dst_ref, used_ref,i + 1, copy_into(1 - slot), lambda c: c.start())

    row = _mod_row(i, SORT_TL)
    g2 = mod_ref[pl.ds(row, 1), 5 * D:6 * D]
    lane = lax.broadcasted_iota(jnp.int32, (SORT_TL, CAP), 1)
    comb = (jnp.where(lane == lp_ref[:, 0:1], w_ref[:, 0:1], 0.0)
            + jnp.where(lane == lp_ref[:, 1:2], w_ref[:, 1:2], 0.0)).astype(BF16)

    _run_copies(dst_ref, used_ref,i, copy_into(slot), lambda c: c.wait())

    def clear(j, carry):
        yl_ref[slot, pl.ds(pl.multiple_of(j * CHUNK, CHUNK), CHUNK), :] = jnp.zeros((CHUNK, D), BF16)
        return carry

    lax.fori_loop(used_ref[i], N_CHUNK, clear, 0)
    y = jnp.dot(comb, yl_ref[slot], preferred_element_type=F32)
    o_ref[...] = x1_ref[...] + g2 * y


def _combine(dst, used, lpos_tok, w_tok, x1, mod_l, ys):
    tok = lambda w: pl.BlockSpec((SORT_TL, w), lambda i, *_: (i, 0))
    return pl.pallas_call(
        _combine_body,
        out_shape=jax.ShapeDtypeStruct((T_ALL, D), F32),
        grid_spec=pltpu.PrefetchScalarGridSpec(
            num_scalar_prefetch=2,
            grid=(N_SORT,),
            in_specs=[tok(2), tok(2), tok(D),
                      pl.BlockSpec((MOD_ROWS, 6 * D), lambda i, *_: (0, 0)),
                      pl.BlockSpec(memory_space=pl.ANY)],
            out_specs=tok(D),
            scratch_shapes=[pltpu.VMEM((2, CAP, D), BF16), pltpu.SemaphoreType.DMA((2,))],
        ),
        compiler_params=pltpu.CompilerParams(dimension_semantics=("arbitrary",), vmem_limit_bytes=VMEM_LIMIT),
        name="combine",
    )(dst, used, lpos_tok, w_tok, x1, mod_l, ys)


def _final_norm_body(x_ref, g_ref, o_ref):
    x = x_ref[...]
    ms = jnp.mean(x * x, axis=-1, keepdims=True)
    o_ref[...] = x * lax.rsqrt(ms + EPS) * g_ref[...]


def _final_norm(x, gain, row0, rows):
    tm = 512
    first = row0 // tm
    return pl.pallas_call(
        _final_norm_body,
        out_shape=jax.ShapeDtypeStruct((rows, D), F32),
        grid=(rows // tm,),
        in_specs=[pl.BlockSpec((tm, D), lambda i: (first + i, 0)), pl.BlockSpec((1, D), lambda i: (0, 0))],
        out_specs=pl.BlockSpec((tm, D), lambda i: (i, 0)),
        name="final_norm",
    )(x, gain)


def _rope_tables():
    pos = jnp.arange(LAT_L)
    rowp = (pos // GRID_W).astype(F32)
    colp = (pos % GRID_W).astype(F32)
    pairs = HD // 4
    inv = ROPE_BASE ** (-jnp.arange(pairs, dtype=F32) / pairs)
    ang = jnp.concatenate([rowp[:, None] * inv] * 2 + [colp[:, None] * inv] * 2, axis=-1)
    cos = jnp.tile(jnp.cos(ang), (1, N_HEADS))
    sin = jnp.tile(jnp.sin(ang), (1, N_HEADS))
    cos = jnp.concatenate([jnp.ones((TM_IN, ATTN_W), F32), cos], axis=0)
    sin = jnp.concatenate([jnp.zeros((TM_IN, ATTN_W), F32), sin], axis=0)
    return cos, sin


def _head_matrices():
    i = jnp.arange(ATTN_W)
    bd = (i[:, None] // HD == i[None, :] // HD).astype(BF16)
    half = HD // 4
    j = i[None, :]
    src = i[:, None]
    first = (j % (2 * half)) < half
    rm = jnp.where(first & (src == j + half), -1.0, 0.0) + jnp.where(~first & (src == j - half), 1.0, 0.0)
    return bd, rm.astype(BF16)


def kernel(x_prompt, x_sample, cache_k, cache_v, c, c_ctx, w_ada, b_ada, norm1, norm2, w_in, q_norm, k_norm,
           sink, w_attn_proj, dw_w, dw_b, cln_g, cln_b, w_conv_proj, w_out, w_router, router_bias,
           w_e_gate, w_e_up, w_e_down, final_norm):
    x = jnp.concatenate([x_prompt.reshape(T_CTX, D), x_sample.reshape(T_LAT, D)], axis=0)
    cond = jnp.concatenate([c_ctx[None, :], c, jnp.zeros((MOD_ROWS - 1 - LAT_B, D), F32)], axis=0)
    mod = _ada_table(cond, w_ada, b_ada)
    cos_t, sin_t = _rope_tables()
    bd, rm = _head_matrices()
    w_in_b = w_in.astype(BF16)
    wap_b = w_attn_proj.astype(BF16)
    wcp_b = w_conv_proj.astype(BF16)
    wo_b = w_out.astype(BF16)
    wr_t = w_router.T
    rb = router_bias.reshape(N_EXP, 1)
    ck = cache_k.reshape(LAT_B, DEPTH, PAST, KV_W)
    cv = cache_v.reshape(LAT_B, DEPTH, PAST, KV_W)

    new_k, new_v = [], []
    for l in range(DEPTH):
        q, k, v, u, gates = _inproj(x, mod[l], norm1[l][None, :], w_in_b[l],
                                    jnp.tile(q_norm[l], N_HEADS)[None, :], jnp.tile(k_norm[l], N_KV)[None, :],
                                    cos_t, sin_t, bd, rm)
        new_k.append(k[:T_CTX].reshape(CTX_B, CTX_L, N_KV, HD))
        new_v.append(v[:T_CTX].reshape(CTX_B, CTX_L, N_KV, HD))
        o_ctx = _attn_ctx(sink[l], q, k, v)
        o_lat = _attn_lat(sink[l], q, k, v, ck, cv, l)
        conv_o = _conv(u, dw_w[l], dw_b[l][None, :], cln_g[l][None, :], cln_b[l][None, :])
        x1, h, logits_t = _merge(o_ctx, o_lat, conv_o, gates, x, mod[l], norm2[l][None, :],
                                 wap_b[l], wcp_b[l], wo_b[l], wr_t)
        ri, rw, cnt = _router(logits_t, rb)
        lpos, dst, used, starts, ends = _sort_tables(ri, cnt)
        rows_ch = ends[-1] // CHUNK
        tail = jnp.stack([rows_ch, XS_ROWS // CHUNK - rows_ch]).astype(jnp.int32)
        lpos8 = jnp.concatenate([lpos, jnp.zeros((6, T_ALL), jnp.int32)], axis=0)
        xs = _sort_scatter(dst, used, tail, lpos8, h)
        ys = _ffn(_ffn_units(starts, ends), xs, w_e_gate, w_e_up, w_e_down, l)
        x = _combine(dst, used, lpos.T, rw[0:2].T, x1, mod[l], ys)

    gain = final_norm[None, :]
    y_prompt = _final_norm(x, gain, 0, T_CTX).reshape(CTX_B, CTX_L, D)
    y_sample = _final_norm(x, gain, T_CTX, T_LAT).reshape(LAT_B, LAT_L, D)
    return y_prompt, y_sample, jnp.stack(new_k, axis=1), jnp.stack(new_v, axis=1)
```

```python
import jax
import jax.numpy as jnp
from jax import lax
from jax.experimental import pallas as pl
from jax.experimental.pallas import tpu as pltpu

F32 = jnp.float32
BF16 = jnp.bfloat16

D = 1024
DEPTH = 4
CTX_B, CTX_L = 32, 256
LAT_B, LAT_L = 8, 1024
PAST = 512
T_CTX = CTX_B * CTX_L
T_LAT = LAT_B * LAT_L
T_ALL = T_CTX + T_LAT
GRID_W = 64
HD = 64
N_HEADS = 8
N_KV = 2
ATTN_W = N_HEADS * HD
KV_W = N_KV * HD
WINDOW = 128
LAT_QB = 256
CONV_W = 512
CONV_K = 31
CONV_PAD = CONV_K // 2
N_EXP = 16
N_GROUPS = 4
EXP_PER_GROUP = N_EXP // N_GROUPS
D_EXP = 512
IN_W = ATTN_W + 2 * KV_W + 2 * CONV_W + 2 * D
OFF_K = ATTN_W
OFF_V = ATTN_W + KV_W
OFF_A = ATTN_W + 2 * KV_W
OFF_B = OFF_A + CONV_W
OFF_G = OFF_A + 2 * CONV_W
EPS = 1e-6
NEG = -1e30
ROPE_BASE = 10000.0
MOD_ROWS = 16

TM_IN = 512
TM_MERGE = 512
TM_ROUTE = 2048
ROUTE_CHUNK = 256
TM_CONV = 256
CONV_HALO = 16
SORT_TL = 512
N_SORT = T_ALL // SORT_TL
CHUNK = 16
CAP = 2 * SORT_TL + 2 * 128
N_CHUNK = CAP // CHUNK
TM_FFN = 512
FFN_SUB = 256
XS_ROWS = 2 * T_ALL + N_SORT * N_EXP * (CHUNK - 1)
N_FFN_TILES = XS_ROWS // TM_FFN
N_UNITS = N_FFN_TILES + N_EXP - 1

VMEM_LIMIT = 56 * 1024 * 1024


def _sigmoid(x):
    return 1.0 / (1.0 + jnp.exp(-x))


def _mod_row(tile, tm):
    start = tile * tm
    return jnp.where(start < T_CTX, 0, 1 + (start - T_CTX) // LAT_L)


def _stream_specs(tm, lat_off):
    n_ctx = T_CTX // tm
    return (pl.BlockSpec((tm, D), lambda i, *_: (jnp.minimum(i, n_ctx - 1), 0)),
            pl.BlockSpec((tm, D), lambda i, *_: (jnp.maximum(i - n_ctx, 0) + lat_off, 0)))


def _ada_body(cond_ref, w_ref, b_ref, o_ref):
    c = cond_ref[...]
    s = (c * _sigmoid(c)).astype(BF16)
    o_ref[0] = jnp.dot(s, w_ref[0].astype(BF16), preferred_element_type=F32) + b_ref[0]


def _ada_table(cond, w_ada, b_ada):
    nj = 6 * D // 1024
    return pl.pallas_call(
        _ada_body,
        out_shape=jax.ShapeDtypeStruct((DEPTH, MOD_ROWS, 6 * D), F32),
        grid=(DEPTH, nj),
        in_specs=[
            pl.BlockSpec((MOD_ROWS, D), lambda l, j: (0, 0)),
            pl.BlockSpec((1, D, 1024), lambda l, j: (l, 0, j)),
            pl.BlockSpec((1, 1, 1024), lambda l, j: (l, 0, j)),
        ],
        out_specs=pl.BlockSpec((1, MOD_ROWS, 1024), lambda l, j: (l, 0, j)),
        name="ada_table",
    )(cond, w_ada, b_ada.reshape(DEPTH, 1, 6 * D))


def _head_norm_rope(y, gain, cos, sin, bd, rm):
    ss = jnp.dot((y * y).astype(BF16), bd, preferred_element_type=F32)
    yn = y * lax.rsqrt(ss * (1.0 / HD) + EPS) * gain
    rot = jnp.dot(yn.astype(BF16), rm, preferred_element_type=F32)
    return yn * cos + rot * sin


def _inproj_body(xc_ref, xl_ref, mod_ref, n1_ref, w_ref, qg_ref, kg_ref, cos_ref, sin_ref, bd_ref, rm_ref,
                 q_ref, k_ref, v_ref, u_ref, g_ref):
    step = pl.program_id(0)
    row = _mod_row(step, TM_IN)
    sh = mod_ref[pl.ds(row, 1), 0:D]
    sc = mod_ref[pl.ds(row, 1), D:2 * D]
    x = jnp.where(step < T_CTX // TM_IN, xc_ref[...], xl_ref[...])
    ms = jnp.mean(x * x, axis=-1, keepdims=True)
    h = ((x * lax.rsqrt(ms + EPS) * n1_ref[...]) * (1.0 + sc) + sh).astype(BF16)

    def proj(lo, hi):
        return jnp.dot(h, w_ref[:, lo:hi], preferred_element_type=F32)

    cos = cos_ref[...]
    sin = sin_ref[...]
    q = _head_norm_rope(proj(0, OFF_K), qg_ref[...], cos, sin, bd_ref[...], rm_ref[...])
    q_ref[...] = (q * (HD ** -0.5)).astype(BF16)
    k = _head_norm_rope(proj(OFF_K, OFF_V), kg_ref[...], cos[:, :KV_W], sin[:, :KV_W],
                        bd_ref[:KV_W, :KV_W], rm_ref[:KV_W, :KV_W])
    k_ref[...] = k
    v_ref[...] = proj(OFF_V, OFF_A)
    a = proj(OFF_A, OFF_B)
    b = proj(OFF_B, OFF_G)
    u_ref[...] = (a * _sigmoid(b)).astype(BF16)
    for j in range(2):
        g = proj(OFF_G + j * D, OFF_G + (j + 1) * D)
        g_ref[:, j * D:(j + 1) * D] = _sigmoid(g).astype(BF16)


def _inproj(x_ctx, x_lat, lat_off, mod_l, n1, w_in, qg, kg, cos_t, sin_t, bd, rm):
    n_ctx_tiles = T_CTX // TM_IN
    tiles_per_seq = LAT_L // TM_IN

    def tab(i):
        return (jnp.where(i < n_ctx_tiles, 0, 1 + (i - n_ctx_tiles) % tiles_per_seq), 0)

    full = lambda shape: pl.BlockSpec(shape, lambda i: (0,) * len(shape))
    row = lambda w: pl.BlockSpec((TM_IN, w), lambda i: (i, 0))
    return pl.pallas_call(
        _inproj_body,
        out_shape=(
            jax.ShapeDtypeStruct((T_ALL, ATTN_W), BF16),
            jax.ShapeDtypeStruct((T_ALL, KV_W), F32),
            jax.ShapeDtypeStruct((T_ALL, KV_W), F32),
            jax.ShapeDtypeStruct((T_ALL, CONV_W), BF16),
            jax.ShapeDtypeStruct((T_ALL, 2 * D), BF16),
        ),
        grid=(T_ALL // TM_IN,),
        in_specs=[
            *_stream_specs(TM_IN, lat_off), full((MOD_ROWS, 6 * D)), full((1, D)), full((D, IN_W)),
            full((1, ATTN_W)), full((1, KV_W)),
            pl.BlockSpec((TM_IN, ATTN_W), tab), pl.BlockSpec((TM_IN, ATTN_W), tab),
            full((ATTN_W, ATTN_W)), full((ATTN_W, ATTN_W)),
        ],
        out_specs=(row(ATTN_W), row(KV_W), row(KV_W), row(CONV_W), row(2 * D)),
        compiler_params=pltpu.CompilerParams(dimension_semantics=("arbitrary",), vmem_limit_bytes=VMEM_LIMIT),
        name="inproj",
    )(x_ctx, x_lat, mod_l, n1, w_in, qg, kg, cos_t, sin_t, bd, rm)


def _head_pair_kv(k, v):
    lane = lax.broadcasted_iota(jnp.int32, k.shape, 1)
    low = lane < HD
    zero = jnp.zeros_like(k)
    k_sw = pltpu.roll(k, HD, 1)
    v_sw = pltpu.roll(v, HD, 1)
    g0 = (jnp.where(low, k, zero), jnp.where(low, v, zero), jnp.where(low, zero, k_sw), jnp.where(low, zero, v_sw))
    g1 = (jnp.where(low, k_sw, zero), jnp.where(low, v_sw, zero), jnp.where(low, zero, k), jnp.where(low, zero, v))
    return tuple(tuple(t.astype(BF16) for t in g) for g in (g0, g1))


def _sink_attend(qp, kk, vv, sink, mask):
    s = lax.dot_general(qp, kk, (((1,), (1,)), ((), ())), preferred_element_type=F32)
    if mask is not None:
        s = jnp.where(mask, s, NEG)
    m = jnp.maximum(jnp.max(s, axis=-1, keepdims=True), sink)
    p = jnp.exp(s - m)
    den = jnp.sum(p, axis=-1, keepdims=True) + jnp.exp(sink - m)
    return jnp.dot(p.astype(BF16), vv, preferred_element_type=F32) / den


def _attend_per_pair(sink_ref, q_ref, o_ref, kv):
    for pair in range(N_HEADS // 2):
        k_lo, v_lo, k_hi, v_hi = kv[pair // 2]
        qp = q_ref[:, pair * 128:(pair + 1) * 128]
        o = (_sink_attend(qp, k_lo, v_lo, sink_ref[2 * pair], None)
             + _sink_attend(qp, k_hi, v_hi, sink_ref[2 * pair + 1], None))
        o_ref[:, pair * 128:(pair + 1) * 128] = o.astype(BF16)


def _attend_stacked(sink_ref, q_ref, o_ref, kv, mask):
    nq = q_ref.shape[0]
    mask = jnp.concatenate([mask, mask], axis=0)
    for g in range(N_KV):
        k_lo, v_lo, k_hi, v_hi = kv[g]
        q2 = jnp.concatenate([q_ref[:, (2 * g) * 128:(2 * g + 1) * 128],
                              q_ref[:, (2 * g + 1) * 128:(2 * g + 2) * 128]], axis=0)

        def sinks(parity):
            top = jnp.full((nq, 1), sink_ref[4 * g + parity], F32)
            bot = jnp.full((nq, 1), sink_ref[4 * g + 2 + parity], F32)
            return jnp.concatenate([top, bot], axis=0)

        o = _sink_attend(q2, k_lo, v_lo, sinks(0), mask) + _sink_attend(q2, k_hi, v_hi, sinks(1), mask)
        o_ref[:, (2 * g) * 128:(2 * g + 1) * 128] = o[:nq].astype(BF16)
        o_ref[:, (2 * g + 1) * 128:(2 * g + 2) * 128] = o[nq:].astype(BF16)


def _attn_ctx_body(sink_ref, q_ref, k_ref, v_ref, o_ref):
    _attend_per_pair(sink_ref, q_ref, o_ref, _head_pair_kv(k_ref[...], v_ref[...]))


def _attn_ctx(sink_l, q, k, v):
    blk = lambda w: pl.BlockSpec((CTX_L, w), lambda b: (b, 0))
    return pl.pallas_call(
        _attn_ctx_body,
        out_shape=jax.ShapeDtypeStruct((T_CTX, ATTN_W), BF16),
        grid=(CTX_B,),
        in_specs=[pl.BlockSpec(memory_space=pltpu.SMEM), blk(ATTN_W), blk(KV_W), blk(KV_W)],
        out_specs=blk(ATTN_W),
        compiler_params=pltpu.CompilerParams(dimension_semantics=("arbitrary",)),
        name="attn_ctx",
    )(sink_l, q, k, v)


def _attn_lat_body(sink_ref, q_ref, kp_ref, kc_ref, kn_ref, vp_ref, vc_ref, vn_ref, ck_ref, cv_ref, o_ref):
    i = pl.program_id(1)
    nblk = LAT_L // LAT_QB
    k = jnp.concatenate([kp_ref[...], kc_ref[...], kn_ref[...], ck_ref[0, 0]], axis=0)
    v = jnp.concatenate([vp_ref[...], vc_ref[...], vn_ref[...], cv_ref[0, 0]], axis=0)
    nloc = LAT_QB + 2 * WINDOW
    nk = nloc + PAST
    r = lax.broadcasted_iota(jnp.int32, (LAT_QB, nk), 0)
    c = lax.broadcasted_iota(jnp.int32, (LAT_QB, nk), 1)
    local = (c - r >= 0) & (c - r <= 2 * WINDOW)
    local = local & ((c >= WINDOW) | (i > 0)) & ((c < WINDOW + LAT_QB) | (i < nblk - 1))
    mask = local | (c >= nloc)
    _attend_stacked(sink_ref, q_ref, o_ref, _head_pair_kv(k, v), mask)


def _attn_lat(sink_l, q, k, v, cache_k, cache_v, layer):
    nblk = LAT_L // LAT_QB
    per = LAT_QB // WINDOW
    nhalo = LAT_L // WINDOW
    base = T_CTX // LAT_QB
    hbase = T_CTX // WINDOW
    cur = lambda b, i: (base + b * nblk + i, 0)
    prev = lambda b, i: (hbase + b * nhalo + jnp.maximum(i * per - 1, 0), 0)
    nxt = lambda b, i: (hbase + b * nhalo + jnp.minimum((i + 1) * per, nhalo - 1), 0)
    kvb = lambda rows, f: pl.BlockSpec((rows, KV_W), f)
    cache = pl.BlockSpec((1, 1, PAST, KV_W), lambda b, i: (b, layer, 0, 0))
    return pl.pallas_call(
        _attn_lat_body,
        out_shape=jax.ShapeDtypeStruct((T_LAT, ATTN_W), BF16),
        grid=(LAT_B, nblk),
        in_specs=[pl.BlockSpec(memory_space=pltpu.SMEM), pl.BlockSpec((LAT_QB, ATTN_W), cur),
                  kvb(WINDOW, prev), kvb(LAT_QB, cur), kvb(WINDOW, nxt),
                  kvb(WINDOW, prev), kvb(LAT_QB, cur), kvb(WINDOW, nxt), cache, cache],
        out_specs=pl.BlockSpec((LAT_QB, ATTN_W), lambda b, i: (b * nblk + i, 0)),
        compiler_params=pltpu.CompilerParams(dimension_semantics=("arbitrary", "arbitrary")),
        name="attn_lat",
    )(sink_l, q, k, k, k, v, v, v, cache_k, cache_v)


def _conv_body(up_ref, uc_ref, un_ref, w_ref, b_ref, g_ref, beta_ref, o_ref, pad_ref):
    i = pl.program_id(0)
    n_ctx = T_CTX // TM_CONV
    per_seq = LAT_L // TM_CONV
    j = (i - n_ctx) % per_seq
    has_prev = (i >= n_ctx) & (j > 0)
    has_next = (i >= n_ctx) & (j < per_seq - 1)
    zero = jnp.zeros((CONV_HALO, CONV_W), F32)
    pad_ref[0, 0:CONV_HALO, :] = jnp.where(has_prev, up_ref[...].astype(F32), zero)
    pad_ref[0, CONV_HALO:CONV_HALO + TM_CONV, :] = uc_ref[...].astype(F32)
    pad_ref[0, CONV_HALO + TM_CONV:, :] = jnp.where(has_next, un_ref[...].astype(F32), zero)
    n_sh = TM_CONV + 2 * CONV_HALO - 8
    for s in range(1, 8):
        pad_ref[s, 0:n_sh, :] = pad_ref[0, s:s + n_sh, :]
    chunk = 64
    for c0 in range(0, TM_CONV, chunk):
        acc = jnp.zeros((chunk, CONV_W), F32) + b_ref[...]
        for t in range(CONV_K):
            src = CONV_HALO - CONV_PAD + t
            row = src - src % 8 + c0
            acc = acc + pad_ref[src % 8, row:row + chunk, :] * w_ref[t:t + 1, :]
        mu = jnp.mean(acc, axis=-1, keepdims=True)
        cen = acc - mu
        var = jnp.mean(cen * cen, axis=-1, keepdims=True)
        y = cen * lax.rsqrt(var + EPS) * g_ref[...] + beta_ref[...]
        o_ref[c0:c0 + chunk, :] = (y * _sigmoid(y)).astype(BF16)


def _conv(u, dw_w, dw_b, ln_g, ln_b):
    per = TM_CONV // CONV_HALO
    last = T_ALL // CONV_HALO - 1
    full = lambda shape: pl.BlockSpec(shape, lambda i: (0,) * len(shape))
    return pl.pallas_call(
        _conv_body,
        out_shape=jax.ShapeDtypeStruct((T_ALL, CONV_W), BF16),
        grid=(T_ALL // TM_CONV,),
        in_specs=[
            pl.BlockSpec((CONV_HALO, CONV_W), lambda i: (jnp.maximum(i * per - 1, 0), 0)),
            pl.BlockSpec((TM_CONV, CONV_W), lambda i: (i, 0)),
            pl.BlockSpec((CONV_HALO, CONV_W), lambda i: (jnp.minimum((i + 1) * per, last), 0)),
            full((CONV_K, CONV_W)), full((1, CONV_W)), full((1, CONV_W)), full((1, CONV_W)),
        ],
        out_specs=pl.BlockSpec((TM_CONV, CONV_W), lambda i: (i, 0)),
        scratch_shapes=[pltpu.VMEM((8, TM_CONV + 2 * CONV_HALO, CONV_W), F32)],
        compiler_params=pltpu.CompilerParams(dimension_semantics=("arbitrary",)),
        name="conv_module",
    )(u, u, u, dw_w, dw_b, ln_g, ln_b)


def _row_pick(idx, rows):
    out = jnp.zeros_like(rows[0])
    for e, r in enumerate(rows):
        out = jnp.where(idx == e, r, out)
    return out


def _argmax_rows(rows):
    best = rows[0]
    idx = jnp.zeros(rows[0].shape, jnp.int32)
    for e in range(1, len(rows)):
        upd = rows[e] > best
        idx = jnp.where(upd, e, idx)
        best = jnp.where(upd, rows[e], best)
    return idx


def _route(logits_t, bias):
    scores = _sigmoid(logits_t)
    sel = scores + bias
    s_rows = [scores[e:e + 1, :] for e in range(N_EXP)]
    rows = [sel[e:e + 1, :] for e in range(N_EXP)]
    grp = []
    for g in range(N_GROUPS):
        r = rows[g * EXP_PER_GROUP:(g + 1) * EXP_PER_GROUP]
        best = None
        for a in range(EXP_PER_GROUP):
            for b in range(a + 1, EXP_PER_GROUP):
                s = r[a] + r[b]
                best = s if best is None else jnp.maximum(best, s)
        grp.append(best)
    gidx = _argmax_rows(grp)
    masked = [jnp.where(gidx == e // EXP_PER_GROUP, rows[e], NEG) for e in range(N_EXP)]
    i1 = _argmax_rows(masked)
    i2 = _argmax_rows([jnp.where(i1 == e, -jnp.inf, masked[e]) for e in range(N_EXP)])
    w1 = _row_pick(i1, s_rows)
    w2 = _row_pick(i2, s_rows)
    tot = w1 + w2
    return i1, i2, w1 / tot, w2 / tot


def _merge_body(ac_ref, al_ref, c_ref, g_ref, xc_ref, xl_ref, mod_ref, n2_ref, wap_ref, wcp_ref, wo_ref, wr_ref,
                x1_ref, h_ref, lg_ref):
    step = pl.program_id(0)
    row = _mod_row(step, TM_MERGE)
    g1 = mod_ref[pl.ds(row, 1), 2 * D:3 * D]
    sh2 = mod_ref[pl.ds(row, 1), 3 * D:4 * D]
    sc2 = mod_ref[pl.ds(row, 1), 4 * D:5 * D]
    is_ctx = step < T_CTX // TM_MERGE
    a = jnp.where(is_ctx, ac_ref[...], al_ref[...])
    attn = jnp.dot(a, wap_ref[...], preferred_element_type=F32)
    conv = jnp.dot(c_ref[...], wcp_ref[...], preferred_element_type=F32)
    merged = g_ref[:, 0:D].astype(F32) * attn + g_ref[:, D:2 * D].astype(F32) * conv
    mix = jnp.dot(merged.astype(BF16), wo_ref[...], preferred_element_type=F32)
    x1 = jnp.where(is_ctx, xc_ref[...], xl_ref[...]) + g1 * mix
    x1_ref[...] = x1
    ms = jnp.mean(x1 * x1, axis=-1, keepdims=True)
    h = (x1 * lax.rsqrt(ms + EPS) * n2_ref[...]) * (1.0 + sc2) + sh2
    h_ref[...] = h.astype(BF16)
    lg_ref[...] = lax.dot_general(wr_ref[...], h, (((1,), (1,)), ((), ())),
                                  precision=lax.Precision.HIGHEST, preferred_element_type=F32)


def _merge(attn_ctx, attn_lat, conv_o, gates, x_ctx, x_lat, lat_off, mod_l, n2, wap, wcp, wo, wr_t):
    tm = TM_MERGE
    n_ctx = T_CTX // tm
    full = lambda shape: pl.BlockSpec(shape, lambda i: (0,) * len(shape))
    row = lambda w: pl.BlockSpec((tm, w), lambda i: (i, 0))
    a_ctx = pl.BlockSpec((tm, ATTN_W), lambda i: (jnp.minimum(i, n_ctx - 1), 0))
    a_lat = pl.BlockSpec((tm, ATTN_W), lambda i: (jnp.maximum(i - n_ctx, 0), 0))
    return pl.pallas_call(
        _merge_body,
        out_shape=(
            jax.ShapeDtypeStruct((T_ALL, D), F32),
            jax.ShapeDtypeStruct((T_ALL, D), BF16),
            jax.ShapeDtypeStruct((N_EXP, T_ALL), F32),
        ),
        grid=(T_ALL // tm,),
        in_specs=[a_ctx, a_lat, row(CONV_W), row(2 * D), *_stream_specs(tm, lat_off),
                  full((MOD_ROWS, 6 * D)), full((1, D)),
                  full((ATTN_W, D)), full((CONV_W, D)), full((D, D)), full((N_EXP, D))],
        out_specs=(row(D), row(D), pl.BlockSpec((N_EXP, tm), lambda i: (0, i))),
        compiler_params=pltpu.CompilerParams(dimension_semantics=("arbitrary",), vmem_limit_bytes=VMEM_LIMIT),
        name="merge",
    )(attn_ctx, attn_lat, conv_o, gates, x_ctx, x_lat, mod_l, n2, wap, wcp, wo, wr_t)


def _router_body(lg_ref, rb_ref, lp_ref, tok_ref, cnt_ref):
    tm = TM_ROUTE
    i1, i2, w1, w2 = _route(lg_ref[...], rb_ref[...])

    eio = lax.broadcasted_iota(jnp.int32, (N_EXP, tm), 0)
    oh1 = eio == i1
    oh2 = eio == i2
    oh = (oh1 | oh2).astype(F32)
    nc = ROUTE_CHUNK
    tri = (lax.broadcasted_iota(jnp.int32, (nc, nc), 0) < lax.broadcasted_iota(jnp.int32, (nc, nc), 1)).astype(BF16)
    lane = lax.broadcasted_iota(jnp.int32, (N_EXP, 128), 1)
    esub = lax.broadcasted_iota(jnp.int32, (N_EXP, 1), 0)
    counts = jnp.zeros((N_EXP, 128), F32)
    pos = []
    for s0 in range(0, tm, SORT_TL):
        base = jnp.zeros((N_EXP, 1), F32)
        ranks = []
        for c0 in range(s0, s0 + SORT_TL, nc):
            ohc = oh[:, c0:c0 + nc]
            ranks.append(base + jnp.dot(ohc.astype(BF16), tri, preferred_element_type=F32))
            base = base + jnp.sum(ohc, axis=1, keepdims=True)
        counts = jnp.where(lane == s0 // SORT_TL, base, counts)
        padded = jnp.floor((base + (CHUNK - 1)) * (1.0 / CHUNK)) * CHUNK
        run_start = jnp.zeros((N_EXP, 1), F32)
        for e in range(N_EXP - 1):
            run_start = run_start + jnp.where(esub > e, padded[e:e + 1, :], 0.0)
        pos.append(jnp.concatenate(ranks, axis=1) + run_start)
    pos = jnp.concatenate(pos, axis=1)
    p1 = jnp.sum(jnp.where(oh1, pos, 0.0), axis=0, keepdims=True)
    p2 = jnp.sum(jnp.where(oh2, pos, 0.0), axis=0, keepdims=True)
    cnt_ref[0] = counts

    sub = lax.broadcasted_iota(jnp.int32, (8, tm), 0)
    lp_ref[...] = jnp.where(sub == 0, p1.astype(jnp.int32), jnp.where(sub == 1, p2.astype(jnp.int32), 0))
    rows = jnp.where(sub == 0, p1, jnp.where(sub == 1, p2, jnp.where(sub == 2, w1, jnp.where(sub == 3, w2, 0.0))))
    tok_ref[...] = jnp.transpose(rows)


def _router(logits_t, rb):
    tm = TM_ROUTE
    lanes = lambda rows: pl.BlockSpec((rows, tm), lambda i: (0, i))
    return pl.pallas_call(
        _router_body,
        out_shape=(
            jax.ShapeDtypeStruct((8, T_ALL), jnp.int32),
            jax.ShapeDtypeStruct((T_ALL, 8), F32),
            jax.ShapeDtypeStruct((T_ALL // tm, N_EXP, 128), F32),
        ),
        grid=(T_ALL // tm,),
        in_specs=[lanes(N_EXP), pl.BlockSpec((N_EXP, 1), lambda i: (0, 0))],
        out_specs=(lanes(8), pl.BlockSpec((tm, 8), lambda i: (i, 0)),
                   pl.BlockSpec((1, N_EXP, 128), lambda i: (i, 0, 0))),
        compiler_params=pltpu.CompilerParams(dimension_semantics=("arbitrary",)),
        name="router",
    )(logits_t, rb)


def _sort_tables(cnt):
    per = TM_ROUTE // SORT_TL
    c = jnp.transpose(cnt[:, :, :per], (0, 2, 1)).reshape(N_SORT, N_EXP).astype(jnp.int32)
    pc = (c + (CHUNK - 1)) // CHUNK * CHUNK
    loff = jnp.cumsum(pc, axis=1) - pc
    used = jnp.sum(pc, axis=1)
    seg = jnp.sum(pc, axis=0)
    ends = jnp.cumsum(seg)
    starts = ends - seg
    gstart = starts[None, :] + jnp.cumsum(pc, axis=0) - pc
    j = jnp.arange(N_CHUNK, dtype=jnp.int32)[None, :, None]
    lo = (loff // CHUNK)[:, None, :]
    in_run = (j >= lo) & (j < lo + (pc // CHUNK)[:, None, :])
    dst = jnp.sum(jnp.where(in_run, (gstart // CHUNK)[:, None, :] + j - lo, 0), axis=-1)
    return (dst.reshape(-1).astype(jnp.int32), (used // CHUNK).astype(jnp.int32),
            starts.astype(jnp.int32), ends.astype(jnp.int32))


def _run_copies(dst_ref, used_ref, tile, make_copy, act):
    def body(j, carry):
        act(make_copy(j, dst_ref[tile * N_CHUNK + j]))
        return carry

    lax.fori_loop(0, used_ref[tile], body, 0)


def _sort_body(dst_ref, used_ref, tail_ref, lp_ref, h_ref, xs_hbm, xl_ref, sem):
    i = pl.program_id(0)
    slot = i % 2
    last = pl.num_programs(0) - 1
    r = lax.broadcasted_iota(jnp.int32, (CAP, SORT_TL), 0)
    place = ((r == lp_ref[0:1, :]) | (r == lp_ref[1:2, :])).astype(BF16)
    xl_ref[slot] = jnp.dot(place, h_ref[...], preferred_element_type=F32).astype(BF16)

    def copy_from(buf):
        def make(src, dst):
            return pltpu.make_async_copy(xl_ref.at[buf, pl.ds(src * CHUNK, CHUNK), :],
                                         xs_hbm.at[pl.ds(dst * CHUNK, CHUNK), :], sem.at[buf])
        return make

    def tail_copies(act):
        def body(j, carry):
            act(copy_from(slot)(N_CHUNK - 1, tail_ref[0] + j))
            return carry
        lax.fori_loop(0, tail_ref[1], body, 0)

    _run_copies(dst_ref, used_ref,i, copy_from(slot), lambda c: c.start())

    @pl.when(i == last)
    def _():
        tail_copies(lambda c: c.start())

    @pl.when(i > 0)
    def _():
        _run_copies(dst_ref, used_ref,i - 1, copy_from(1 - slot), lambda c: c.wait())

    @pl.when(i == last)
    def _():
        _run_copies(dst_ref, used_ref,i, copy_from(slot), lambda c: c.wait())
        tail_copies(lambda c: c.wait())


def _sort_scatter(dst, used, tail, lpos8, h):
    return pl.pallas_call(
        _sort_body,
        out_shape=jax.ShapeDtypeStruct((XS_ROWS, D), BF16),
        grid_spec=pltpu.PrefetchScalarGridSpec(
            num_scalar_prefetch=3,
            grid=(N_SORT,),
            in_specs=[pl.BlockSpec((8, SORT_TL), lambda i, *_: (0, i)),
                      pl.BlockSpec((SORT_TL, D), lambda i, *_: (i, 0))],
            out_specs=pl.BlockSpec(memory_space=pl.ANY),
            scratch_shapes=[pltpu.VMEM((2, CAP, D), BF16), pltpu.SemaphoreType.DMA((2,))],
        ),
        compiler_params=pltpu.CompilerParams(dimension_semantics=("arbitrary",), vmem_limit_bytes=VMEM_LIMIT),
        name="sort_scatter",
    )(dst, used, tail, lpos8, h)


def _ffn_body(tile_ref, exp_ref, lo_ref, hi_ref, first_ref, mode_ref, fresh_ref,
              x_ref, wg_ref, wu_ref, wd_ref, o_ref, wg_b, wu_b, wd_b):
    u = pl.program_id(0)

    @pl.when(mode_ref[u] == 2)
    def _():
        o_ref[...] = jnp.zeros_like(o_ref)

    @pl.when(fresh_ref[u] == 1)
    def _():
        wg_b[...] = wg_ref[0, 0].astype(BF16)
        wu_b[...] = wu_ref[0, 0].astype(BF16)
        wd_b[...] = wd_ref[0, 0].astype(BF16)

    @pl.when(mode_ref[u] == 1)
    def _():
        parts = []
        for r0 in range(0, TM_FFN, FFN_SUB):
            x = x_ref[r0:r0 + FFN_SUB, :]
            g = jnp.dot(x, wg_b[...], preferred_element_type=F32)
            up = jnp.dot(x, wu_b[...], preferred_element_type=F32)
            rows = r0 + lax.broadcasted_iota(jnp.int32, (FFN_SUB, 1), 0)
            mine = (rows >= lo_ref[u]) & (rows < hi_ref[u])
            hid = jnp.where(mine, g * _sigmoid(g) * up, 0.0).astype(BF16)
            parts.append(jnp.dot(hid, wd_b[...], preferred_element_type=F32).astype(BF16))
        y = jnp.concatenate(parts, axis=0)

        @pl.when(first_ref[u] == 1)
        def _():
            o_ref[...] = y

        @pl.when(first_ref[u] == 0)
        def _():
            o_ref[...] += y


def _ffn(units, xs, wg, wu, wd, layer):
    xmap = lambda u, tile, *_: (tile[u], 0)
    wmap = lambda u, tile, exp, *_: (layer, exp[u], 0, 0)
    return pl.pallas_call(
        _ffn_body,
        out_shape=jax.ShapeDtypeStruct((XS_ROWS, D), BF16),
        grid_spec=pltpu.PrefetchScalarGridSpec(
            num_scalar_prefetch=7,
            grid=(N_UNITS,),
            in_specs=[pl.BlockSpec((TM_FFN, D), xmap),
                      pl.BlockSpec((1, 1, D, D_EXP), wmap), pl.BlockSpec((1, 1, D, D_EXP), wmap),
                      pl.BlockSpec((1, 1, D_EXP, D), wmap)],
            out_specs=pl.BlockSpec((TM_FFN, D), xmap),
            scratch_shapes=[pltpu.VMEM((D, D_EXP), BF16), pltpu.VMEM((D, D_EXP), BF16), pltpu.VMEM((D_EXP, D), BF16)],
        ),
        compiler_params=pltpu.CompilerParams(dimension_semantics=("arbitrary",), vmem_limit_bytes=VMEM_LIMIT),
        name="expert_ffn",
    )(*units, xs, wg, wu, wd)


def _ffn_units(starts, ends):
    total_rows = ends[-1]
    t0 = jnp.arange(N_FFN_TILES, dtype=jnp.int32) * TM_FFN
    t1 = jnp.minimum(t0 + TM_FFN, total_rows) - 1
    e_first = jnp.sum(ends[None, :] <= t0[:, None], axis=1).astype(jnp.int32)
    e_last = jnp.sum(ends[None, :] <= t1[:, None], axis=1).astype(jnp.int32)
    n_per = jnp.where(t0 < total_rows, e_last - e_first + 1, 0)
    u_end = jnp.cumsum(n_per)
    u_start = u_end - n_per
    total = u_end[-1]
    u = jnp.arange(N_UNITS, dtype=jnp.int32)
    uc = jnp.minimum(u, total - 1)
    tile = jnp.sum(u_end[None, :] <= uc[:, None], axis=1).astype(jnp.int32)
    exp = e_first[tile] + (uc - u_start[tile])
    lo = jnp.clip(starts[exp] - tile * TM_FFN, 0, TM_FFN)
    hi = jnp.clip(ends[exp] - tile * TM_FFN, 0, TM_FFN)
    first = ((uc == u_start[tile]) & (u < total)).astype(jnp.int32)
    exp = exp.astype(jnp.int32)
    fresh = jnp.concatenate([jnp.ones((1,), jnp.int32), (exp[1:] != exp[:-1]).astype(jnp.int32)])
    spare = jnp.sum(t0 < total_rows).astype(jnp.int32) + (u - total)
    mode = jnp.where(u < total, 1, jnp.where(spare < N_FFN_TILES, 2, 0)).astype(jnp.int32)
    tile = jnp.where(u < total, tile, jnp.minimum(spare, N_FFN_TILES - 1)).astype(jnp.int32)
    return tile, exp, lo.astype(jnp.int32), hi.astype(jnp.int32), first, mode, fresh


def _combine_body(dst_ref, used_ref, tok_ref, x1_ref, mod_ref, ys_hbm, o_ref, yl_ref, sem):
    i = pl.program_id(0)
    slot = i % 2
    last = pl.num_programs(0) - 1

    def copy_into(buf):
        def make(loc, glob):
            return pltpu.make_async_copy(ys_hbm.at[pl.ds(glob * CHUNK, CHUNK), :],
                                         yl_ref.at[buf, pl.ds(loc * CHUNK, CHUNK), :], sem.at[buf])
        return make

    @pl.when(i == 0)
    def _():
        _run_copies(dst_ref, used_ref,i, copy_into(slot), lambda c: c.start())

    @pl.when(i < last)
    def _():
        _run_copies(dst_ref, used_ref,i + 1, copy_into(1 - slot), lambda c: c.start())

    row = _mod_row(i, SORT_TL)
    g2 = mod_ref[pl.ds(row, 1), 5 * D:6 * D]
    lane = lax.broadcasted_iota(jnp.int32, (SORT_TL, CAP), 1)
    slot1 = tok_ref[:, 0:1].astype(jnp.int32)
    slot2 = tok_ref[:, 1:2].astype(jnp.int32)
    comb = (jnp.where(lane == slot1, tok_ref[:, 2:3], 0.0)
            + jnp.where(lane == slot2, tok_ref[:, 3:4], 0.0)).astype(BF16)

    _run_copies(dst_ref, used_ref,i, copy_into(slot), lambda c: c.wait())

    def clear(j, carry):
        yl_ref[slot, pl.ds(pl.multiple_of(j * CHUNK, CHUNK), CHUNK), :] = jnp.zeros((CHUNK, D), BF16)
        return carry

    lax.fori_loop(used_ref[i], N_CHUNK, clear, 0)
    y = jnp.dot(comb, yl_ref[slot], preferred_element_type=F32)
    o_ref[...] = x1_ref[...] + g2 * y


def _combine(dst, used, tok8, x1, mod_l, ys):
    tok = lambda w: pl.BlockSpec((SORT_TL, w), lambda i, *_: (i, 0))
    return pl.pallas_call(
        _combine_body,
        out_shape=jax.ShapeDtypeStruct((T_ALL, D), F32),
        grid_spec=pltpu.PrefetchScalarGridSpec(
            num_scalar_prefetch=2,
            grid=(N_SORT,),
            in_specs=[tok(8), tok(D),
                      pl.BlockSpec((MOD_ROWS, 6 * D), lambda i, *_: (0, 0)),
                      pl.BlockSpec(memory_space=pl.ANY)],
            out_specs=tok(D),
            scratch_shapes=[pltpu.VMEM((2, CAP, D), BF16), pltpu.SemaphoreType.DMA((2,))],
        ),
        compiler_params=pltpu.CompilerParams(dimension_semantics=("arbitrary",), vmem_limit_bytes=VMEM_LIMIT),
        name="combine",
    )(dst, used, tok8, x1, mod_l, ys)


def _final_norm_body(x_ref, g_ref, o_ref):
    x = x_ref[...]
    ms = jnp.mean(x * x, axis=-1, keepdims=True)
    o_ref[...] = x * lax.rsqrt(ms + EPS) * g_ref[...]


def _final_norm(x, gain, row0, rows):
    tm = 512
    first = row0 // tm
    return pl.pallas_call(
        _final_norm_body,
        out_shape=jax.ShapeDtypeStruct((rows, D), F32),
        grid=(rows // tm,),
        in_specs=[pl.BlockSpec((tm, D), lambda i: (first + i, 0)), pl.BlockSpec((1, D), lambda i: (0, 0))],
        out_specs=pl.BlockSpec((tm, D), lambda i: (i, 0)),
        name="final_norm",
    )(x, gain)


def _rope_tables():
    pos = jnp.arange(LAT_L)
    rowp = (pos // GRID_W).astype(F32)
    colp = (pos % GRID_W).astype(F32)
    pairs = HD // 4
    inv = ROPE_BASE ** (-jnp.arange(pairs, dtype=F32) / pairs)
    ang = jnp.concatenate([rowp[:, None] * inv] * 2 + [colp[:, None] * inv] * 2, axis=-1)
    cos = jnp.tile(jnp.cos(ang), (1, N_HEADS))
    sin = jnp.tile(jnp.sin(ang), (1, N_HEADS))
    cos = jnp.concatenate([jnp.ones((TM_IN, ATTN_W), F32), cos], axis=0)
    sin = jnp.concatenate([jnp.zeros((TM_IN, ATTN_W), F32), sin], axis=0)
    return cos, sin


def _head_matrices():
    i = jnp.arange(ATTN_W)
    bd = (i[:, None] // HD == i[None, :] // HD).astype(BF16)
    half = HD // 4
    j = i[None, :]
    src = i[:, None]
    first = (j % (2 * half)) < half
    rm = jnp.where(first & (src == j + half), -1.0, 0.0) + jnp.where(~first & (src == j - half), 1.0, 0.0)
    return bd, rm.astype(BF16)


def kernel(x_prompt, x_sample, cache_k, cache_v, c, c_ctx, w_ada, b_ada, norm1, norm2, w_in, q_norm, k_norm,
           sink, w_attn_proj, dw_w, dw_b, cln_g, cln_b, w_conv_proj, w_out, w_router, router_bias,
           w_e_gate, w_e_up, w_e_down, final_norm):
    x_ctx, x_lat = x_prompt.reshape(T_CTX, D), x_sample.reshape(T_LAT, D)
    cond = jnp.concatenate([c_ctx[None, :], c, jnp.zeros((MOD_ROWS - 1 - LAT_B, D), F32)], axis=0)
    mod = _ada_table(cond, w_ada, b_ada)
    cos_t, sin_t = _rope_tables()
    bd, rm = _head_matrices()
    w_in_b = w_in.astype(BF16)
    wap_b = w_attn_proj.astype(BF16)
    wcp_b = w_conv_proj.astype(BF16)
    wo_b = w_out.astype(BF16)
    wr_t = w_router.T
    rb = router_bias.reshape(N_EXP, 1)
    ck = cache_k.reshape(LAT_B, DEPTH, PAST, KV_W)
    cv = cache_v.reshape(LAT_B, DEPTH, PAST, KV_W)

    new_k, new_v = [], []
    for l in range(DEPTH):
        merged = l > 0
        q, k, v, u, gates = _inproj(x_ctx, x_lat, T_CTX // TM_IN if merged else 0, mod[l], norm1[l][None, :], w_in_b[l],
                                    jnp.tile(q_norm[l], N_HEADS)[None, :], jnp.tile(k_norm[l], N_KV)[None, :],
                                    cos_t, sin_t, bd, rm)
        new_k.append(k[:T_CTX].reshape(CTX_B, CTX_L, N_KV, HD))
        new_v.append(v[:T_CTX].reshape(CTX_B, CTX_L, N_KV, HD))
        o_ctx = _attn_ctx(sink[l], q, k, v)
        o_lat = _attn_lat(sink[l], q, k, v, ck, cv, l)
        conv_o = _conv(u, dw_w[l], dw_b[l][None, :], cln_g[l][None, :], cln_b[l][None, :])
        x1, h, logits_t = _merge(o_ctx, o_lat, conv_o, gates, x_ctx, x_lat, T_CTX // TM_MERGE if merged else 0,
                                 mod[l], norm2[l][None, :], wap_b[l], wcp_b[l], wo_b[l], wr_t)
        lpos, tok, cnt = _router(logits_t, rb)
        dst, used, starts, ends = _sort_tables(cnt)
        rows_ch = ends[-1] // CHUNK
        tail = jnp.stack([rows_ch, XS_ROWS // CHUNK - rows_ch]).astype(jnp.int32)
        xs = _sort_scatter(dst, used, tail, lpos, h)
        ys = _ffn(_ffn_units(starts, ends), xs, w_e_gate, w_e_up, w_e_down, l)
        x_ctx = x_lat = _combine(dst, used, tok, x1, mod[l], ys)

    gain = final_norm[None, :]
    y_prompt = _final_norm(x_ctx, gain, 0, T_CTX).reshape(CTX_B, CTX_L, D)
    y_sample = _final_norm(x_lat, gain, T_CTX, T_LAT).reshape(LAT_B, LAT_L, D)
    return y_prompt, y_sample, jnp.stack(new_k, axis=1), jnp.stack(new_v, axis=1)
```

```python
import jax
import jax.numpy as jnp
from jax import lax
from jax.experimental import pallas as pl
from jax.experimental.pallas import tpu as pltpu

F32 = jnp.float32
BF16 = jnp.bfloat16

D = 1024
DEPTH = 4
CTX_B, CTX_L = 32, 256
LAT_B, LAT_L = 8, 1024
PAST = 512
T_CTX = CTX_B * CTX_L
T_LAT = LAT_B * LAT_L
T_ALL = T_CTX + T_LAT
GRID_W = 64
HD = 64
N_HEADS = 8
N_KV = 2
ATTN_W = N_HEADS * HD
KV_W = N_KV * HD
WINDOW = 128
LAT_QB = 256
CONV_W = 512
CONV_K = 31
CONV_PAD = CONV_K // 2
N_EXP = 16
N_GROUPS = 4
EXP_PER_GROUP = N_EXP // N_GROUPS
D_EXP = 512
IN_W = ATTN_W + 2 * KV_W + 2 * CONV_W + 2 * D
OFF_K = ATTN_W
OFF_V = ATTN_W + KV_W
OFF_A = ATTN_W + 2 * KV_W
OFF_B = OFF_A + CONV_W
OFF_G = OFF_A + 2 * CONV_W
EPS = 1e-6
NEG = -1e30
ROPE_BASE = 10000.0
MOD_ROWS = 16

TM_IN = 512
TM_MERGE = 512
TM_ROUTE = 2048
ROUTE_CHUNK = 256
TM_CONV = 256
CONV_HALO = 16
SORT_TL = 512
N_SORT = T_ALL // SORT_TL
CHUNK = 16
CAP = 2 * SORT_TL + 2 * 128
N_CHUNK = CAP // CHUNK
TM_FFN = 512
FFN_SUB = 256
XS_ROWS = 2 * T_ALL + N_SORT * N_EXP * (CHUNK - 1)
N_FFN_TILES = XS_ROWS // TM_FFN
N_UNITS = N_FFN_TILES + N_EXP - 1

VMEM_LIMIT = 56 * 1024 * 1024


def _sigmoid(x):
    return 1.0 / (1.0 + jnp.exp(-x))


def _mod_row(tile, tm):
    start = tile * tm
    return jnp.where(start < T_CTX, 0, 1 + (start - T_CTX) // LAT_L)


def _stream_specs(tm, lat_off):
    n_ctx = T_CTX // tm
    return (pl.BlockSpec((tm, D), lambda i, *_: (jnp.minimum(i, n_ctx - 1), 0)),
            pl.BlockSpec((tm, D), lambda i, *_: (jnp.maximum(i - n_ctx, 0) + lat_off, 0)))


def _ada_body(cond_ref, w_ref, b_ref, o_ref):
    c = cond_ref[...]
    s = (c * _sigmoid(c)).astype(BF16)
    o_ref[0] = jnp.dot(s, w_ref[0].astype(BF16), preferred_element_type=F32) + b_ref[0]


def _ada_table(cond, w_ada, b_ada):
    nj = 6 * D // 1024
    return pl.pallas_call(
        _ada_body,
        out_shape=jax.ShapeDtypeStruct((DEPTH, MOD_ROWS, 6 * D), F32),
        grid=(DEPTH, nj),
        in_specs=[
            pl.BlockSpec((MOD_ROWS, D), lambda l, j: (0, 0)),
            pl.BlockSpec((1, D, 1024), lambda l, j: (l, 0, j)),
            pl.BlockSpec((1, 1, 1024), lambda l, j: (l, 0, j)),
        ],
        out_specs=pl.BlockSpec((1, MOD_ROWS, 1024), lambda l, j: (l, 0, j)),
        name="ada_table",
    )(cond, w_ada, b_ada.reshape(DEPTH, 1, 6 * D))


def _head_norm_rope(y, gain, cos, sin, bd, rm):
    ss = jnp.dot((y * y).astype(BF16), bd, preferred_element_type=F32)
    yn = y * lax.rsqrt(ss * (1.0 / HD) + EPS) * gain
    rot = jnp.dot(yn.astype(BF16), rm, preferred_element_type=F32)
    return yn * cos + rot * sin


def _inproj_body(xc_ref, xl_ref, mod_ref, n1_ref, w_ref, qg_ref, kg_ref, cos_ref, sin_ref, bd_ref, rm_ref,
                 q_ref, k_ref, v_ref, u_ref, g_ref, kc_ref, vc_ref):
    step = pl.program_id(0)
    row = _mod_row(step, TM_IN)
    sh = mod_ref[pl.ds(row, 1), 0:D]
    sc = mod_ref[pl.ds(row, 1), D:2 * D]
    x = jnp.where(step < T_CTX // TM_IN, xc_ref[...], xl_ref[...])
    ms = jnp.mean(x * x, axis=-1, keepdims=True)
    h = ((x * lax.rsqrt(ms + EPS) * n1_ref[...]) * (1.0 + sc) + sh).astype(BF16)

    def proj(lo, hi):
        return jnp.dot(h, w_ref[:, lo:hi], preferred_element_type=F32)

    cos = cos_ref[...]
    sin = sin_ref[...]
    q = _head_norm_rope(proj(0, OFF_K), qg_ref[...], cos, sin, bd_ref[...], rm_ref[...])
    q_ref[...] = (q * (HD ** -0.5)).astype(BF16)
    k = _head_norm_rope(proj(OFF_K, OFF_V), kg_ref[...], cos[:, :KV_W], sin[:, :KV_W],
                        bd_ref[:KV_W, :KV_W], rm_ref[:KV_W, :KV_W])
    v = proj(OFF_V, OFF_A)
    k_ref[...] = k
    v_ref[...] = v

    @pl.when(step < T_CTX // TM_IN)
    def _():
        kc_ref[...] = k
        vc_ref[...] = v

    a = proj(OFF_A, OFF_B)
    b = proj(OFF_B, OFF_G)
    u_ref[...] = (a * _sigmoid(b)).astype(BF16)
    for j in range(2):
        g = proj(OFF_G + j * D, OFF_G + (j + 1) * D)
        g_ref[:, j * D:(j + 1) * D] = _sigmoid(g).astype(BF16)


def _inproj(x_ctx, x_lat, lat_off, mod_l, n1, w_in, qg, kg, cos_t, sin_t, bd, rm):
    n_ctx_tiles = T_CTX // TM_IN
    tiles_per_seq = LAT_L // TM_IN

    def tab(i):
        return (jnp.where(i < n_ctx_tiles, 0, 1 + (i - n_ctx_tiles) % tiles_per_seq), 0)

    full = lambda shape: pl.BlockSpec(shape, lambda i: (0,) * len(shape))
    row = lambda w: pl.BlockSpec((TM_IN, w), lambda i: (i, 0))
    ctx_kv = pl.BlockSpec((TM_IN, KV_W), lambda i: (jnp.minimum(i, n_ctx_tiles - 1), 0))
    return pl.pallas_call(
        _inproj_body,
        out_shape=(
            jax.ShapeDtypeStruct((T_ALL, ATTN_W), BF16),
            jax.ShapeDtypeStruct((T_ALL, KV_W), F32),
            jax.ShapeDtypeStruct((T_ALL, KV_W), F32),
            jax.ShapeDtypeStruct((T_ALL, CONV_W), BF16),
            jax.ShapeDtypeStruct((T_ALL, 2 * D), BF16),
            jax.ShapeDtypeStruct((T_CTX, KV_W), F32),
            jax.ShapeDtypeStruct((T_CTX, KV_W), F32),
        ),
        grid=(T_ALL // TM_IN,),
        in_specs=[
            *_stream_specs(TM_IN, lat_off), full((MOD_ROWS, 6 * D)), full((1, D)), full((D, IN_W)),
            full((1, ATTN_W)), full((1, KV_W)),
            pl.BlockSpec((TM_IN, ATTN_W), tab), pl.BlockSpec((TM_IN, ATTN_W), tab),
            full((ATTN_W, ATTN_W)), full((ATTN_W, ATTN_W)),
        ],
        out_specs=(row(ATTN_W), row(KV_W), row(KV_W), row(CONV_W), row(2 * D), ctx_kv, ctx_kv),
        compiler_params=pltpu.CompilerParams(dimension_semantics=("arbitrary",), vmem_limit_bytes=VMEM_LIMIT),
        name="inproj",
    )(x_ctx, x_lat, mod_l, n1, w_in, qg, kg, cos_t, sin_t, bd, rm)


def _head_pair_kv(k, v):
    lane = lax.broadcasted_iota(jnp.int32, k.shape, 1)
    low = lane < HD
    zero = jnp.zeros_like(k)
    k_sw = pltpu.roll(k, HD, 1)
    v_sw = pltpu.roll(v, HD, 1)
    g0 = (jnp.where(low, k, zero), jnp.where(low, v, zero), jnp.where(low, zero, k_sw), jnp.where(low, zero, v_sw))
    g1 = (jnp.where(low, k_sw, zero), jnp.where(low, v_sw, zero), jnp.where(low, zero, k), jnp.where(low, zero, v))
    return tuple(tuple(t.astype(BF16) for t in g) for g in (g0, g1))


def _sink_attend(qp, kk, vv, sink, mask):
    s = lax.dot_general(qp, kk, (((1,), (1,)), ((), ())), preferred_element_type=F32)
    if mask is not None:
        s = jnp.where(mask, s, NEG)
    m = jnp.maximum(jnp.max(s, axis=-1, keepdims=True), sink)
    p = jnp.exp(s - m)
    den = jnp.sum(p, axis=-1, keepdims=True) + jnp.exp(sink - m)
    return jnp.dot(p.astype(BF16), vv, preferred_element_type=F32) / den


def _attend_per_pair(sink_ref, q_ref, o_ref, kv):
    for pair in range(N_HEADS // 2):
        k_lo, v_lo, k_hi, v_hi = kv[pair // 2]
        qp = q_ref[:, pair * 128:(pair + 1) * 128]
        o = (_sink_attend(qp, k_lo, v_lo, sink_ref[2 * pair], None)
             + _sink_attend(qp, k_hi, v_hi, sink_ref[2 * pair + 1], None))
        o_ref[:, pair * 128:(pair + 1) * 128] = o.astype(BF16)


def _attend_stacked(sink_ref, q_ref, o_ref, kv, mask):
    nq = q_ref.shape[0]
    mask = jnp.concatenate([mask, mask], axis=0)
    for g in range(N_KV):
        k_lo, v_lo, k_hi, v_hi = kv[g]
        q2 = jnp.concatenate([q_ref[:, (2 * g) * 128:(2 * g + 1) * 128],
                              q_ref[:, (2 * g + 1) * 128:(2 * g + 2) * 128]], axis=0)

        def sinks(parity):
            top = jnp.full((nq, 1), sink_ref[4 * g + parity], F32)
            bot = jnp.full((nq, 1), sink_ref[4 * g + 2 + parity], F32)
            return jnp.concatenate([top, bot], axis=0)

        o = _sink_attend(q2, k_lo, v_lo, sinks(0), mask) + _sink_attend(q2, k_hi, v_hi, sinks(1), mask)
        o_ref[:, (2 * g) * 128:(2 * g + 1) * 128] = o[:nq].astype(BF16)
        o_ref[:, (2 * g + 1) * 128:(2 * g + 2) * 128] = o[nq:].astype(BF16)


def _attn_ctx_body(sink_ref, q_ref, k_ref, v_ref, o_ref):
    _attend_per_pair(sink_ref, q_ref, o_ref, _head_pair_kv(k_ref[...], v_ref[...]))


def _attn_ctx(sink_l, q, k, v):
    blk = lambda w: pl.BlockSpec((CTX_L, w), lambda b: (b, 0))
    return pl.pallas_call(
        _attn_ctx_body,
        out_shape=jax.ShapeDtypeStruct((T_CTX, ATTN_W), BF16),
        grid=(CTX_B,),
        in_specs=[pl.BlockSpec(memory_space=pltpu.SMEM), blk(ATTN_W), blk(KV_W), blk(KV_W)],
        out_specs=blk(ATTN_W),
        compiler_params=pltpu.CompilerParams(dimension_semantics=("arbitrary",)),
        name="attn_ctx",
    )(sink_l, q, k, v)


def _attn_lat_body(sink_ref, q_ref, kp_ref, kc_ref, kn_ref, vp_ref, vc_ref, vn_ref, ck_ref, cv_ref, o_ref):
    i = pl.program_id(1)
    nblk = LAT_L // LAT_QB
    k = jnp.concatenate([kp_ref[...], kc_ref[...], kn_ref[...], ck_ref[0, 0]], axis=0)
    v = jnp.concatenate([vp_ref[...], vc_ref[...], vn_ref[...], cv_ref[0, 0]], axis=0)
    nloc = LAT_QB + 2 * WINDOW
    nk = nloc + PAST
    r = lax.broadcasted_iota(jnp.int32, (LAT_QB, nk), 0)
    c = lax.broadcasted_iota(jnp.int32, (LAT_QB, nk), 1)
    local = (c - r >= 0) & (c - r <= 2 * WINDOW)
    local = local & ((c >= WINDOW) | (i > 0)) & ((c < WINDOW + LAT_QB) | (i < nblk - 1))
    mask = local | (c >= nloc)
    _attend_stacked(sink_ref, q_ref, o_ref, _head_pair_kv(k, v), mask)


def _attn_lat(sink_l, q, k, v, cache_k, cache_v, layer):
    nblk = LAT_L // LAT_QB
    per = LAT_QB // WINDOW
    nhalo = LAT_L // WINDOW
    base = T_CTX // LAT_QB
    hbase = T_CTX // WINDOW
    cur = lambda b, i: (base + b * nblk + i, 0)
    prev = lambda b, i: (hbase + b * nhalo + jnp.maximum(i * per - 1, 0), 0)
    nxt = lambda b, i: (hbase + b * nhalo + jnp.minimum((i + 1) * per, nhalo - 1), 0)
    kvb = lambda rows, f: pl.BlockSpec((rows, KV_W), f)
    cache = pl.BlockSpec((1, 1, PAST, KV_W), lambda b, i: (b, layer, 0, 0))
    return pl.pallas_call(
        _attn_lat_body,
        out_shape=jax.ShapeDtypeStruct((T_LAT, ATTN_W), BF16),
        grid=(LAT_B, nblk),
        in_specs=[pl.BlockSpec(memory_space=pltpu.SMEM), pl.BlockSpec((LAT_QB, ATTN_W), cur),
                  kvb(WINDOW, prev), kvb(LAT_QB, cur), kvb(WINDOW, nxt),
                  kvb(WINDOW, prev), kvb(LAT_QB, cur), kvb(WINDOW, nxt), cache, cache],
        out_specs=pl.BlockSpec((LAT_QB, ATTN_W), lambda b, i: (b * nblk + i, 0)),
        compiler_params=pltpu.CompilerParams(dimension_semantics=("arbitrary", "arbitrary")),
        name="attn_lat",
    )(sink_l, q, k, k, k, v, v, v, cache_k, cache_v)


def _conv_body(up_ref, uc_ref, un_ref, w_ref, b_ref, g_ref, beta_ref, o_ref, pad_ref):
    i = pl.program_id(0)
    n_ctx = T_CTX // TM_CONV
    per_seq = LAT_L // TM_CONV
    j = (i - n_ctx) % per_seq
    has_prev = (i >= n_ctx) & (j > 0)
    has_next = (i >= n_ctx) & (j < per_seq - 1)
    zero = jnp.zeros((CONV_HALO, CONV_W), F32)
    pad_ref[0, 0:CONV_HALO, :] = jnp.where(has_prev, up_ref[...].astype(F32), zero)
    pad_ref[0, CONV_HALO:CONV_HALO + TM_CONV, :] = uc_ref[...].astype(F32)
    pad_ref[0, CONV_HALO + TM_CONV:, :] = jnp.where(has_next, un_ref[...].astype(F32), zero)
    n_sh = TM_CONV + 2 * CONV_HALO - 8
    for s in range(1, 8):
        pad_ref[s, 0:n_sh, :] = pad_ref[0, s:s + n_sh, :]
    chunk = 64
    for c0 in range(0, TM_CONV, chunk):
        acc = jnp.zeros((chunk, CONV_W), F32) + b_ref[...]
        for t in range(CONV_K):
            src = CONV_HALO - CONV_PAD + t
            row = src - src % 8 + c0
            acc = acc + pad_ref[src % 8, row:row + chunk, :] * w_ref[t:t + 1, :]
        mu = jnp.mean(acc, axis=-1, keepdims=True)
        cen = acc - mu
        var = jnp.mean(cen * cen, axis=-1, keepdims=True)
        y = cen * lax.rsqrt(var + EPS) * g_ref[...] + beta_ref[...]
        o_ref[c0:c0 + chunk, :] = (y * _sigmoid(y)).astype(BF16)


def _conv(u, dw_w, dw_b, ln_g, ln_b):
    per = TM_CONV // CONV_HALO
    last = T_ALL // CONV_HALO - 1
    full = lambda shape: pl.BlockSpec(shape, lambda i: (0,) * len(shape))
    return pl.pallas_call(
        _conv_body,
        out_shape=jax.ShapeDtypeStruct((T_ALL, CONV_W), BF16),
        grid=(T_ALL // TM_CONV,),
        in_specs=[
            pl.BlockSpec((CONV_HALO, CONV_W), lambda i: (jnp.maximum(i * per - 1, 0), 0)),
            pl.BlockSpec((TM_CONV, CONV_W), lambda i: (i, 0)),
            pl.BlockSpec((CONV_HALO, CONV_W), lambda i: (jnp.minimum((i + 1) * per, last), 0)),
            full((CONV_K, CONV_W)), full((1, CONV_W)), full((1, CONV_W)), full((1, CONV_W)),
        ],
        out_specs=pl.BlockSpec((TM_CONV, CONV_W), lambda i: (i, 0)),
        scratch_shapes=[pltpu.VMEM((8, TM_CONV + 2 * CONV_HALO, CONV_W), F32)],
        compiler_params=pltpu.CompilerParams(dimension_semantics=("arbitrary",)),
        name="conv_module",
    )(u, u, u, dw_w, dw_b, ln_g, ln_b)


def _row_pick(idx, rows):
    out = jnp.zeros_like(rows[0])
    for e, r in enumerate(rows):
        out = jnp.where(idx == e, r, out)
    return out


def _argmax_rows(rows):
    best = rows[0]
    idx = jnp.zeros(rows[0].shape, jnp.int32)
    for e in range(1, len(rows)):
        upd = rows[e] > best
        idx = jnp.where(upd, e, idx)
        best = jnp.where(upd, rows[e], best)
    return idx


def _route(logits_t, bias):
    scores = _sigmoid(logits_t)
    sel = scores + bias
    s_rows = [scores[e:e + 1, :] for e in range(N_EXP)]
    rows = [sel[e:e + 1, :] for e in range(N_EXP)]
    grp = []
    for g in range(N_GROUPS):
        r = rows[g * EXP_PER_GROUP:(g + 1) * EXP_PER_GROUP]
        best = None
        for a in range(EXP_PER_GROUP):
            for b in range(a + 1, EXP_PER_GROUP):
                s = r[a] + r[b]
                best = s if best is None else jnp.maximum(best, s)
        grp.append(best)
    gidx = _argmax_rows(grp)
    masked = [jnp.where(gidx == e // EXP_PER_GROUP, rows[e], NEG) for e in range(N_EXP)]
    i1 = _argmax_rows(masked)
    i2 = _argmax_rows([jnp.where(i1 == e, -jnp.inf, masked[e]) for e in range(N_EXP)])
    w1 = _row_pick(i1, s_rows)
    w2 = _row_pick(i2, s_rows)
    tot = w1 + w2
    return i1, i2, w1 / tot, w2 / tot


def _merge_body(ac_ref, al_ref, c_ref, g_ref, xc_ref, xl_ref, mod_ref, n2_ref, wap_ref, wcp_ref, wo_ref, wr_ref,
                x1_ref, h_ref, lg_ref):
    step = pl.program_id(0)
    row = _mod_row(step, TM_MERGE)
    g1 = mod_ref[pl.ds(row, 1), 2 * D:3 * D]
    sh2 = mod_ref[pl.ds(row, 1), 3 * D:4 * D]
    sc2 = mod_ref[pl.ds(row, 1), 4 * D:5 * D]
    is_ctx = step < T_CTX // TM_MERGE
    a = jnp.where(is_ctx, ac_ref[...], al_ref[...])
    attn = jnp.dot(a, wap_ref[...], preferred_element_type=F32)
    conv = jnp.dot(c_ref[...], wcp_ref[...], preferred_element_type=F32)
    merged = g_ref[:, 0:D].astype(F32) * attn + g_ref[:, D:2 * D].astype(F32) * conv
    mix = jnp.dot(merged.astype(BF16), wo_ref[...], preferred_element_type=F32)
    x1 = jnp.where(is_ctx, xc_ref[...], xl_ref[...]) + g1 * mix
    x1_ref[...] = x1
    ms = jnp.mean(x1 * x1, axis=-1, keepdims=True)
    h = (x1 * lax.rsqrt(ms + EPS) * n2_ref[...]) * (1.0 + sc2) + sh2
    h_ref[...] = h.astype(BF16)
    lg_ref[...] = lax.dot_general(wr_ref[...], h, (((1,), (1,)), ((), ())),
                                  precision=lax.Precision.HIGHEST, preferred_element_type=F32)


def _merge(attn_ctx, attn_lat, conv_o, gates, x_ctx, x_lat, lat_off, mod_l, n2, wap, wcp, wo, wr_t):
    tm = TM_MERGE
    n_ctx = T_CTX // tm
    full = lambda shape: pl.BlockSpec(shape, lambda i: (0,) * len(shape))
    row = lambda w: pl.BlockSpec((tm, w), lambda i: (i, 0))
    a_ctx = pl.BlockSpec((tm, ATTN_W), lambda i: (jnp.minimum(i, n_ctx - 1), 0))
    a_lat = pl.BlockSpec((tm, ATTN_W), lambda i: (jnp.maximum(i - n_ctx, 0), 0))
    return pl.pallas_call(
        _merge_body,
        out_shape=(
            jax.ShapeDtypeStruct((T_ALL, D), F32),
            jax.ShapeDtypeStruct((T_ALL, D), BF16),
            jax.ShapeDtypeStruct((N_EXP, T_ALL), F32),
        ),
        grid=(T_ALL // tm,),
        in_specs=[a_ctx, a_lat, row(CONV_W), row(2 * D), *_stream_specs(tm, lat_off),
                  full((MOD_ROWS, 6 * D)), full((1, D)),
                  full((ATTN_W, D)), full((CONV_W, D)), full((D, D)), full((N_EXP, D))],
        out_specs=(row(D), row(D), pl.BlockSpec((N_EXP, tm), lambda i: (0, i))),
        compiler_params=pltpu.CompilerParams(dimension_semantics=("arbitrary",), vmem_limit_bytes=VMEM_LIMIT),
        name="merge",
    )(attn_ctx, attn_lat, conv_o, gates, x_ctx, x_lat, mod_l, n2, wap, wcp, wo, wr_t)


def _router_body(lg_ref, rb_ref, lp_ref, tok_ref, cnt_ref):
    tm = TM_ROUTE
    i1, i2, w1, w2 = _route(lg_ref[...], rb_ref[...])

    eio = lax.broadcasted_iota(jnp.int32, (N_EXP, tm), 0)
    oh1 = eio == i1
    oh2 = eio == i2
    oh = (oh1 | oh2).astype(F32)
    nc = ROUTE_CHUNK
    tri = (lax.broadcasted_iota(jnp.int32, (nc, nc), 0) < lax.broadcasted_iota(jnp.int32, (nc, nc), 1)).astype(BF16)
    lane = lax.broadcasted_iota(jnp.int32, (N_EXP, 128), 1)
    esub = lax.broadcasted_iota(jnp.int32, (N_EXP, 1), 0)
    counts = jnp.zeros((N_EXP, 128), F32)
    pos = []
    for s0 in range(0, tm, SORT_TL):
        base = jnp.zeros((N_EXP, 1), F32)
        ranks = []
        for c0 in range(s0, s0 + SORT_TL, nc):
            ohc = oh[:, c0:c0 + nc]
            ranks.append(base + jnp.dot(ohc.astype(BF16), tri, preferred_element_type=F32))
            base = base + jnp.sum(ohc, axis=1, keepdims=True)
        counts = jnp.where(lane == s0 // SORT_TL, base, counts)
        padded = jnp.floor((base + (CHUNK - 1)) * (1.0 / CHUNK)) * CHUNK
        run_start = jnp.zeros((N_EXP, 1), F32)
        for e in range(N_EXP - 1):
            run_start = run_start + jnp.where(esub > e, padded[e:e + 1, :], 0.0)
        pos.append(jnp.concatenate(ranks, axis=1) + run_start)
    pos = jnp.concatenate(pos, axis=1)
    p1 = jnp.sum(jnp.where(oh1, pos, 0.0), axis=0, keepdims=True)
    p2 = jnp.sum(jnp.where(oh2, pos, 0.0), axis=0, keepdims=True)
    cnt_ref[0] = counts

    sub = lax.broadcasted_iota(jnp.int32, (8, tm), 0)
    lp_ref[...] = jnp.where(sub == 0, p1.astype(jnp.int32), jnp.where(sub == 1, p2.astype(jnp.int32), 0))
    rows = jnp.where(sub == 0, p1, jnp.where(sub == 1, p2, jnp.where(sub == 2, w1, jnp.where(sub == 3, w2, 0.0))))
    tok_ref[...] = jnp.transpose(rows)


def _router(logits_t, rb):
    tm = TM_ROUTE
    lanes = lambda rows: pl.BlockSpec((rows, tm), lambda i: (0, i))
    return pl.pallas_call(
        _router_body,
        out_shape=(
            jax.ShapeDtypeStruct((8, T_ALL), jnp.int32),
            jax.ShapeDtypeStruct((T_ALL, 8), F32),
            jax.ShapeDtypeStruct((T_ALL // tm, N_EXP, 128), F32),
        ),
        grid=(T_ALL // tm,),
        in_specs=[lanes(N_EXP), pl.BlockSpec((N_EXP, 1), lambda i: (0, 0))],
        out_specs=(lanes(8), pl.BlockSpec((tm, 8), lambda i: (i, 0)),
                   pl.BlockSpec((1, N_EXP, 128), lambda i: (i, 0, 0))),
        compiler_params=pltpu.CompilerParams(dimension_semantics=("arbitrary",)),
        name="router",
    )(logits_t, rb)


def _sort_tables(cnt):
    per = TM_ROUTE // SORT_TL
    c = jnp.transpose(cnt[:, :, :per], (0, 2, 1)).reshape(N_SORT, N_EXP).astype(jnp.int32)
    pc = (c + (CHUNK - 1)) // CHUNK * CHUNK
    loff = jnp.cumsum(pc, axis=1) - pc
    used = jnp.sum(pc, axis=1)
    seg = jnp.sum(pc, axis=0)
    ends = jnp.cumsum(seg)
    starts = ends - seg
    gstart = starts[None, :] + jnp.cumsum(pc, axis=0) - pc
    j = jnp.arange(N_CHUNK, dtype=jnp.int32)[None, :, None]
    lo = (loff // CHUNK)[:, None, :]
    in_run = (j >= lo) & (j < lo + (pc // CHUNK)[:, None, :])
    dst = jnp.sum(jnp.where(in_run, (gstart // CHUNK)[:, None, :] + j - lo, 0), axis=-1)
    return (dst.reshape(-1).astype(jnp.int32), (used // CHUNK).astype(jnp.int32),
            starts.astype(jnp.int32), ends.astype(jnp.int32))


def _run_copies(dst_ref, used_ref, tile, make_copy, act):
    def body(j, carry):
        act(make_copy(j, dst_ref[tile * N_CHUNK + j]))
        return carry

    lax.fori_loop(0, used_ref[tile], body, 0)


def _sort_body(dst_ref, used_ref, tail_ref, lp_ref, h_ref, xs_hbm, xl_ref, sem):
    i = pl.program_id(0)
    slot = i % 2
    last = pl.num_programs(0) - 1
    r = lax.broadcasted_iota(jnp.int32, (CAP, SORT_TL), 0)
    place = ((r == lp_ref[0:1, :]) | (r == lp_ref[1:2, :])).astype(BF16)
    xl_ref[slot] = jnp.dot(place, h_ref[...], preferred_element_type=F32).astype(BF16)

    def copy_from(buf):
        def make(src, dst):
            return pltpu.make_async_copy(xl_ref.at[buf, pl.ds(src * CHUNK, CHUNK), :],
                                         xs_hbm.at[pl.ds(dst * CHUNK, CHUNK), :], sem.at[buf])
        return make

    def tail_copies(act):
        def body(j, carry):
            act(copy_from(slot)(N_CHUNK - 1, tail_ref[0] + j))
            return carry
        lax.fori_loop(0, tail_ref[1], body, 0)

    _run_copies(dst_ref, used_ref,i, copy_from(slot), lambda c: c.start())

    @pl.when(i == last)
    def _():
        tail_copies(lambda c: c.start())

    @pl.when(i > 0)
    def _():
        _run_copies(dst_ref, used_ref,i - 1, copy_from(1 - slot), lambda c: c.wait())

    @pl.when(i == last)
    def _():
        _run_copies(dst_ref, used_ref,i, copy_from(slot), lambda c: c.wait())
        tail_copies(lambda c: c.wait())


def _sort_scatter(dst, used, tail, lpos8, h):
    return pl.pallas_call(
        _sort_body,
        out_shape=jax.ShapeDtypeStruct((XS_ROWS, D), BF16),
        grid_spec=pltpu.PrefetchScalarGridSpec(
            num_scalar_prefetch=3,
            grid=(N_SORT,),
            in_specs=[pl.BlockSpec((8, SORT_TL), lambda i, *_: (0, i)),
                      pl.BlockSpec((SORT_TL, D), lambda i, *_: (i, 0))],
            out_specs=pl.BlockSpec(memory_space=pl.ANY),
            scratch_shapes=[pltpu.VMEM((2, CAP, D), BF16), pltpu.SemaphoreType.DMA((2,))],
        ),
        compiler_params=pltpu.CompilerParams(dimension_semantics=("arbitrary",), vmem_limit_bytes=VMEM_LIMIT),
        name="sort_scatter",
    )(dst, used, tail, lpos8, h)


def _ffn_body(tile_ref, exp_ref, lo_ref, hi_ref, first_ref, mode_ref, fresh_ref,
              x_ref, wg_ref, wu_ref, wd_ref, o_ref, wg_b, wu_b, wd_b):
    u = pl.program_id(0)

    @pl.when(mode_ref[u] == 2)
    def _():
        o_ref[...] = jnp.zeros_like(o_ref)

    @pl.when(fresh_ref[u] == 1)
    def _():
        wg_b[...] = wg_ref[0, 0].astype(BF16)
        wu_b[...] = wu_ref[0, 0].astype(BF16)
        wd_b[...] = wd_ref[0, 0].astype(BF16)

    @pl.when(mode_ref[u] == 1)
    def _():
        parts = []
        for r0 in range(0, TM_FFN, FFN_SUB):
            x = x_ref[r0:r0 + FFN_SUB, :]
            g = jnp.dot(x, wg_b[...], preferred_element_type=F32)
            up = jnp.dot(x, wu_b[...], preferred_element_type=F32)
            rows = r0 + lax.broadcasted_iota(jnp.int32, (FFN_SUB, 1), 0)
            mine = (rows >= lo_ref[u]) & (rows < hi_ref[u])
            hid = jnp.where(mine, g * _sigmoid(g) * up, 0.0).astype(BF16)
            parts.append(jnp.dot(hid, wd_b[...], preferred_element_type=F32).astype(BF16))
        y = jnp.concatenate(parts, axis=0)

        @pl.when(first_ref[u] == 1)
        def _():
            o_ref[...] = y

        @pl.when(first_ref[u] == 0)
        def _():
            o_ref[...] += y


def _ffn(units, xs, wg, wu, wd, layer):
    xmap = lambda u, tile, *_: (tile[u], 0)
    wmap = lambda u, tile, exp, *_: (layer, exp[u], 0, 0)
    return pl.pallas_call(
        _ffn_body,
        out_shape=jax.ShapeDtypeStruct((XS_ROWS, D), BF16),
        grid_spec=pltpu.PrefetchScalarGridSpec(
            num_scalar_prefetch=7,
            grid=(N_UNITS,),
            in_specs=[pl.BlockSpec((TM_FFN, D), xmap),
                      pl.BlockSpec((1, 1, D, D_EXP), wmap), pl.BlockSpec((1, 1, D, D_EXP), wmap),
                      pl.BlockSpec((1, 1, D_EXP, D), wmap)],
            out_specs=pl.BlockSpec((TM_FFN, D), xmap),
            scratch_shapes=[pltpu.VMEM((D, D_EXP), BF16), pltpu.VMEM((D, D_EXP), BF16), pltpu.VMEM((D_EXP, D), BF16)],
        ),
        compiler_params=pltpu.CompilerParams(dimension_semantics=("arbitrary",), vmem_limit_bytes=VMEM_LIMIT),
        name="expert_ffn",
    )(*units, xs, wg, wu, wd)


def _ffn_units(starts, ends):
    total_rows = ends[-1]
    t0 = jnp.arange(N_FFN_TILES, dtype=jnp.int32) * TM_FFN
    t1 = jnp.minimum(t0 + TM_FFN, total_rows) - 1
    e_first = jnp.sum(ends[None, :] <= t0[:, None], axis=1).astype(jnp.int32)
    e_last = jnp.sum(ends[None, :] <= t1[:, None], axis=1).astype(jnp.int32)
    n_per = jnp.where(t0 < total_rows, e_last - e_first + 1, 0)
    u_end = jnp.cumsum(n_per)
    u_start = u_end - n_per
    total = u_end[-1]
    u = jnp.arange(N_UNITS, dtype=jnp.int32)
    uc = jnp.minimum(u, total - 1)
    tile = jnp.sum(u_end[None, :] <= uc[:, None], axis=1).astype(jnp.int32)
    exp = e_first[tile] + (uc - u_start[tile])
    lo = jnp.clip(starts[exp] - tile * TM_FFN, 0, TM_FFN)
    hi = jnp.clip(ends[exp] - tile * TM_FFN, 0, TM_FFN)
    first = ((uc == u_start[tile]) & (u < total)).astype(jnp.int32)
    exp = exp.astype(jnp.int32)
    fresh = jnp.concatenate([jnp.ones((1,), jnp.int32), (exp[1:] != exp[:-1]).astype(jnp.int32)])
    spare = jnp.sum(t0 < total_rows).astype(jnp.int32) + (u - total)
    mode = jnp.where(u < total, 1, jnp.where(spare < N_FFN_TILES, 2, 0)).astype(jnp.int32)
    tile = jnp.where(u < total, tile, jnp.minimum(spare, N_FFN_TILES - 1)).astype(jnp.int32)
    return tile, exp, lo.astype(jnp.int32), hi.astype(jnp.int32), first, mode, fresh


def _combine_body(dst_ref, used_ref, tok_ref, x1_ref, mod_ref, ys_hbm, o_ref, yl_ref, sem):
    i = pl.program_id(0)
    slot = i % 2
    last = pl.num_programs(0) - 1

    def copy_into(buf):
        def make(loc, glob):
            return pltpu.make_async_copy(ys_hbm.at[pl.ds(glob * CHUNK, CHUNK), :],
                                         yl_ref.at[buf, pl.ds(loc * CHUNK, CHUNK), :], sem.at[buf])
        return make

    @pl.when(i == 0)
    def _():
        _run_copies(dst_ref, used_ref,i, copy_into(slot), lambda c: c.start())

    @pl.when(i < last)
    def _():
        _run_copies(dst_ref, used_ref,i + 1, copy_into(1 - slot), lambda c: c.start())

    row = _mod_row(i, SORT_TL)
    g2 = mod_ref[pl.ds(row, 1), 5 * D:6 * D]
    lane = lax.broadcasted_iota(jnp.int32, (SORT_TL, CAP), 1)
    slot1 = tok_ref[:, 0:1].astype(jnp.int32)
    slot2 = tok_ref[:, 1:2].astype(jnp.int32)
    comb = (jnp.where(lane == slot1, tok_ref[:, 2:3], 0.0)
            + jnp.where(lane == slot2, tok_ref[:, 3:4], 0.0)).astype(BF16)

    _run_copies(dst_ref, used_ref,i, copy_into(slot), lambda c: c.wait())

    def clear(j, carry):
        yl_ref[slot, pl.ds(pl.multiple_of(j * CHUNK, CHUNK), CHUNK), :] = jnp.zeros((CHUNK, D), BF16)
        return carry

    lax.fori_loop(used_ref[i], N_CHUNK, clear, 0)
    y = jnp.dot(comb, yl_ref[slot], preferred_element_type=F32)
    o_ref[...] = x1_ref[...] + g2 * y


def _combine(dst, used, tok8, x1, mod_l, ys):
    tok = lambda w: pl.BlockSpec((SORT_TL, w), lambda i, *_: (i, 0))
    return pl.pallas_call(
        _combine_body,
        out_shape=jax.ShapeDtypeStruct((T_ALL, D), F32),
        grid_spec=pltpu.PrefetchScalarGridSpec(
            num_scalar_prefetch=2,
            grid=(N_SORT,),
            in_specs=[tok(8), tok(D),
                      pl.BlockSpec((MOD_ROWS, 6 * D), lambda i, *_: (0, 0)),
                      pl.BlockSpec(memory_space=pl.ANY)],
            out_specs=tok(D),
            scratch_shapes=[pltpu.VMEM((2, CAP, D), BF16), pltpu.SemaphoreType.DMA((2,))],
        ),
        compiler_params=pltpu.CompilerParams(dimension_semantics=("arbitrary",), vmem_limit_bytes=VMEM_LIMIT),
        name="combine",
    )(dst, used, tok8, x1, mod_l, ys)


def _final_norm_body(x_ref, g_ref, o_ref):
    x = x_ref[...]
    ms = jnp.mean(x * x, axis=-1, keepdims=True)
    o_ref[...] = x * lax.rsqrt(ms + EPS) * g_ref[...]


def _final_norm(x, gain, row0, rows):
    tm = 512
    first = row0 // tm
    return pl.pallas_call(
        _final_norm_body,
        out_shape=jax.ShapeDtypeStruct((rows, D), F32),
        grid=(rows // tm,),
        in_specs=[pl.BlockSpec((tm, D), lambda i: (first + i, 0)), pl.BlockSpec((1, D), lambda i: (0, 0))],
        out_specs=pl.BlockSpec((tm, D), lambda i: (i, 0)),
        name="final_norm",
    )(x, gain)


def _rope_tables():
    pos = jnp.arange(LAT_L)
    rowp = (pos // GRID_W).astype(F32)
    colp = (pos % GRID_W).astype(F32)
    pairs = HD // 4
    inv = ROPE_BASE ** (-jnp.arange(pairs, dtype=F32) / pairs)
    ang = jnp.concatenate([rowp[:, None] * inv] * 2 + [colp[:, None] * inv] * 2, axis=-1)
    cos = jnp.tile(jnp.cos(ang), (1, N_HEADS))
    sin = jnp.tile(jnp.sin(ang), (1, N_HEADS))
    cos = jnp.concatenate([jnp.ones((TM_IN, ATTN_W), F32), cos], axis=0)
    sin = jnp.concatenate([jnp.zeros((TM_IN, ATTN_W), F32), sin], axis=0)
    return cos, sin


def _head_matrices():
    i = jnp.arange(ATTN_W)
    bd = (i[:, None] // HD == i[None, :] // HD).astype(BF16)
    half = HD // 4
    j = i[None, :]
    src = i[:, None]
    first = (j % (2 * half)) < half
    rm = jnp.where(first & (src == j + half), -1.0, 0.0) + jnp.where(~first & (src == j - half), 1.0, 0.0)
    return bd, rm.astype(BF16)


def kernel(x_prompt, x_sample, cache_k, cache_v, c, c_ctx, w_ada, b_ada, norm1, norm2, w_in, q_norm, k_norm,
           sink, w_attn_proj, dw_w, dw_b, cln_g, cln_b, w_conv_proj, w_out, w_router, router_bias,
           w_e_gate, w_e_up, w_e_down, final_norm):
    x_ctx, x_lat = x_prompt.reshape(T_CTX, D), x_sample.reshape(T_LAT, D)
    cond = jnp.concatenate([c_ctx[None, :], c, jnp.zeros((MOD_ROWS - 1 - LAT_B, D), F32)], axis=0)
    mod = _ada_table(cond, w_ada, b_ada)
    cos_t, sin_t = _rope_tables()
    bd, rm = _head_matrices()
    w_in_b = w_in.astype(BF16)
    wap_b = w_attn_proj.astype(BF16)
    wcp_b = w_conv_proj.astype(BF16)
    wo_b = w_out.astype(BF16)
    wr_t = w_router.T
    rb = router_bias.reshape(N_EXP, 1)
    ck = cache_k.reshape(LAT_B, DEPTH, PAST, KV_W)
    cv = cache_v.reshape(LAT_B, DEPTH, PAST, KV_W)

    new_k, new_v = [], []
    for l in range(DEPTH):
        merged = l > 0
        q, k, v, u, gates, k_ctx, v_ctx = _inproj(
            x_ctx, x_lat, T_CTX // TM_IN if merged else 0, mod[l], norm1[l][None, :], w_in_b[l],
            jnp.tile(q_norm[l], N_HEADS)[None, :], jnp.tile(k_norm[l], N_KV)[None, :], cos_t, sin_t, bd, rm)
        new_k.append(k_ctx.reshape(CTX_B, CTX_L, N_KV, HD))
        new_v.append(v_ctx.reshape(CTX_B, CTX_L, N_KV, HD))
        o_ctx = _attn_ctx(sink[l], q, k, v)
        o_lat = _attn_lat(sink[l], q, k, v, ck, cv, l)
        conv_o = _conv(u, dw_w[l], dw_b[l][None, :], cln_g[l][None, :], cln_b[l][None, :])
        x1, h, logits_t = _merge(o_ctx, o_lat, conv_o, gates, x_ctx, x_lat, T_CTX // TM_MERGE if merged else 0,
                                 mod[l], norm2[l][None, :], wap_b[l], wcp_b[l], wo_b[l], wr_t)
        lpos, tok, cnt = _router(logits_t, rb)
        dst, used, starts, ends = _sort_tables(cnt)
        rows_ch = ends[-1] // CHUNK
        tail = jnp.stack([rows_ch, XS_ROWS // CHUNK - rows_ch]).astype(jnp.int32)
        xs = _sort_scatter(dst, used, tail, lpos, h)
        ys = _ffn(_ffn_units(starts, ends), xs, w_e_gate, w_e_up, w_e_down, l)
        x_ctx = x_lat = _combine(dst, used, tok, x1, mod[l], ys)

    gain = final_norm[None, :]
    y_prompt = _final_norm(x_ctx, gain, 0, T_CTX).reshape(CTX_B, CTX_L, D)
    y_sample = _final_norm(x_lat, gain, T_CTX, T_LAT).reshape(LAT_B, LAT_L, D)
    return y_prompt, y_sample, jnp.stack(new_k, axis=1), jnp.stack(new_v, axis=1)
```

```python
import jax
import jax.numpy as jnp
from jax import lax
from jax.experimental import pallas as pl
from jax.experimental.pallas import tpu as pltpu

F32 = jnp.float32
BF16 = jnp.bfloat16

D = 1024
DEPTH = 4
CTX_B, CTX_L = 32, 256
LAT_B, LAT_L = 8, 1024
PAST = 512
T_CTX = CTX_B * CTX_L
T_LAT = LAT_B * LAT_L
T_ALL = T_CTX + T_LAT
GRID_W = 64
HD = 64
N_HEADS = 8
N_KV = 2
ATTN_W = N_HEADS * HD
KV_W = N_KV * HD
WINDOW = 128
LAT_QB = 256
CONV_W = 512
CONV_K = 31
CONV_PAD = CONV_K // 2
N_EXP = 16
N_GROUPS = 4
EXP_PER_GROUP = N_EXP // N_GROUPS
D_EXP = 512
IN_W = ATTN_W + 2 * KV_W + 2 * CONV_W + 2 * D
OFF_K = ATTN_W
OFF_V = ATTN_W + KV_W
OFF_A = ATTN_W + 2 * KV_W
OFF_B = OFF_A + CONV_W
OFF_G = OFF_A + 2 * CONV_W
EPS = 1e-6
NEG = -1e30
ROPE_BASE = 10000.0
MOD_ROWS = 16

TM_IN = 512
TM_MERGE = 512
TM_ROUTE = 2048
ROUTE_CHUNK = 256
TM_CONV = 256
CONV_HALO = 16
SORT_TL = 512
N_SORT = T_ALL // SORT_TL
CHUNK = 16
CAP = 2 * SORT_TL + 2 * 128
N_CHUNK = CAP // CHUNK
TM_FFN = 512
FFN_SUB = 512
XS_ROWS = 2 * T_ALL + N_SORT * N_EXP * (CHUNK - 1)
N_FFN_TILES = XS_ROWS // TM_FFN
N_UNITS = N_FFN_TILES + N_EXP - 1

VMEM_LIMIT = 56 * 1024 * 1024


def _sigmoid(x):
    return 1.0 / (1.0 + jnp.exp(-x))


def _mod_row(tile, tm):
    start = tile * tm
    return jnp.where(start < T_CTX, 0, 1 + (start - T_CTX) // LAT_L)


def _stream_specs(tm, lat_off):
    n_ctx = T_CTX // tm
    return (pl.BlockSpec((tm, D), lambda i, *_: (jnp.minimum(i, n_ctx - 1), 0)),
            pl.BlockSpec((tm, D), lambda i, *_: (jnp.maximum(i - n_ctx, 0) + lat_off, 0)))


def _ada_body(cond_ref, w_ref, b_ref, o_ref):
    c = cond_ref[...]
    s = (c * _sigmoid(c)).astype(BF16)
    o_ref[0] = jnp.dot(s, w_ref[0].astype(BF16), preferred_element_type=F32) + b_ref[0]


def _ada_table(cond, w_ada, b_ada):
    nj = 6 * D // 1024
    return pl.pallas_call(
        _ada_body,
        out_shape=jax.ShapeDtypeStruct((DEPTH, MOD_ROWS, 6 * D), F32),
        grid=(DEPTH, nj),
        in_specs=[
            pl.BlockSpec((MOD_ROWS, D), lambda l, j: (0, 0)),
            pl.BlockSpec((1, D, 1024), lambda l, j: (l, 0, j)),
            pl.BlockSpec((1, 1, 1024), lambda l, j: (l, 0, j)),
        ],
        out_specs=pl.BlockSpec((1, MOD_ROWS, 1024), lambda l, j: (l, 0, j)),
        name="ada_table",
    )(cond, w_ada, b_ada.reshape(DEPTH, 1, 6 * D))


def _head_norm_rope(y, gain, cos, sin, bd, rm):
    ss = jnp.dot((y * y).astype(BF16), bd, preferred_element_type=F32)
    yn = y * lax.rsqrt(ss * (1.0 / HD) + EPS) * gain
    rot = jnp.dot(yn.astype(BF16), rm, preferred_element_type=F32)
    return yn * cos + rot * sin


def _inproj_body(xc_ref, xl_ref, mod_ref, n1_ref, w_ref, qg_ref, kg_ref, cos_ref, sin_ref, bd_ref, rm_ref,
                 q_ref, k_ref, v_ref, u_ref, g_ref):
    step = pl.program_id(0)
    row = _mod_row(step, TM_IN)
    sh = mod_ref[pl.ds(row, 1), 0:D]
    sc = mod_ref[pl.ds(row, 1), D:2 * D]
    x = jnp.where(step < T_CTX // TM_IN, xc_ref[...], xl_ref[...])
    ms = jnp.mean(x * x, axis=-1, keepdims=True)
    h = ((x * lax.rsqrt(ms + EPS) * n1_ref[...]) * (1.0 + sc) + sh).astype(BF16)

    def proj(lo, hi):
        return jnp.dot(h, w_ref[:, lo:hi], preferred_element_type=F32)

    cos = cos_ref[...]
    sin = sin_ref[...]
    q = _head_norm_rope(proj(0, OFF_K), qg_ref[...], cos, sin, bd_ref[...], rm_ref[...])
    q_ref[...] = (q * (HD ** -0.5)).astype(BF16)
    k = _head_norm_rope(proj(OFF_K, OFF_V), kg_ref[...], cos[:, :KV_W], sin[:, :KV_W],
                        bd_ref[:KV_W, :KV_W], rm_ref[:KV_W, :KV_W])
    k_ref[...] = k
    v_ref[...] = proj(OFF_V, OFF_A)
    a = proj(OFF_A, OFF_B)
    b = proj(OFF_B, OFF_G)
    u_ref[...] = (a * _sigmoid(b)).astype(BF16)
    for j in range(2):
        g = proj(OFF_G + j * D, OFF_G + (j + 1) * D)
        g_ref[:, j * D:(j + 1) * D] = _sigmoid(g).astype(BF16)


def _inproj(x_ctx, x_lat, lat_off, mod_l, n1, w_in, qg, kg, cos_t, sin_t, bd, rm):
    n_ctx_tiles = T_CTX // TM_IN
    tiles_per_seq = LAT_L // TM_IN

    def tab(i):
        return (jnp.where(i < n_ctx_tiles, 0, 1 + (i - n_ctx_tiles) % tiles_per_seq), 0)

    full = lambda shape: pl.BlockSpec(shape, lambda i: (0,) * len(shape))
    row = lambda w: pl.BlockSpec((TM_IN, w), lambda i: (i, 0))
    return pl.pallas_call(
        _inproj_body,
        out_shape=(
            jax.ShapeDtypeStruct((T_ALL, ATTN_W), BF16),
            jax.ShapeDtypeStruct((T_ALL, KV_W), F32),
            jax.ShapeDtypeStruct((T_ALL, KV_W), F32),
            jax.ShapeDtypeStruct((T_ALL, CONV_W), BF16),
            jax.ShapeDtypeStruct((T_ALL, 2 * D), BF16),
        ),
        grid=(T_ALL // TM_IN,),
        in_specs=[
            *_stream_specs(TM_IN, lat_off), full((MOD_ROWS, 6 * D)), full((1, D)), full((D, IN_W)),
            full((1, ATTN_W)), full((1, KV_W)),
            pl.BlockSpec((TM_IN, ATTN_W), tab), pl.BlockSpec((TM_IN, ATTN_W), tab),
            full((ATTN_W, ATTN_W)), full((ATTN_W, ATTN_W)),
        ],
        out_specs=(row(ATTN_W), row(KV_W), row(KV_W), row(CONV_W), row(2 * D)),
        compiler_params=pltpu.CompilerParams(dimension_semantics=("arbitrary",), vmem_limit_bytes=VMEM_LIMIT),
        name="inproj",
    )(x_ctx, x_lat, mod_l, n1, w_in, qg, kg, cos_t, sin_t, bd, rm)


def _head_pair_kv(k, v):
    lane = lax.broadcasted_iota(jnp.int32, k.shape, 1)
    low = lane < HD
    zero = jnp.zeros_like(k)
    k_sw = pltpu.roll(k, HD, 1)
    v_sw = pltpu.roll(v, HD, 1)
    g0 = (jnp.where(low, k, zero), jnp.where(low, v, zero), jnp.where(low, zero, k_sw), jnp.where(low, zero, v_sw))
    g1 = (jnp.where(low, k_sw, zero), jnp.where(low, v_sw, zero), jnp.where(low, zero, k), jnp.where(low, zero, v))
    return tuple(tuple(t.astype(BF16) for t in g) for g in (g0, g1))


def _sink_attend(qp, kk, vv, sink, mask):
    s = lax.dot_general(qp, kk, (((1,), (1,)), ((), ())), preferred_element_type=F32)
    if mask is not None:
        s = jnp.where(mask, s, NEG)
    m = jnp.maximum(jnp.max(s, axis=-1, keepdims=True), sink)
    p = jnp.exp(s - m)
    den = jnp.sum(p, axis=-1, keepdims=True) + jnp.exp(sink - m)
    return jnp.dot(p.astype(BF16), vv, preferred_element_type=F32) / den


def _attend_per_pair(sink_ref, q_ref, o_ref, kv):
    for pair in range(N_HEADS // 2):
        k_lo, v_lo, k_hi, v_hi = kv[pair // 2]
        qp = q_ref[:, pair * 128:(pair + 1) * 128]
        o = (_sink_attend(qp, k_lo, v_lo, sink_ref[2 * pair], None)
             + _sink_attend(qp, k_hi, v_hi, sink_ref[2 * pair + 1], None))
        o_ref[:, pair * 128:(pair + 1) * 128] = o.astype(BF16)


def _attend_stacked(sink_ref, q_ref, o_ref, kv, mask):
    nq = q_ref.shape[0]
    mask = jnp.concatenate([mask, mask], axis=0)
    for g in range(N_KV):
        k_lo, v_lo, k_hi, v_hi = kv[g]
        q2 = jnp.concatenate([q_ref[:, (2 * g) * 128:(2 * g + 1) * 128],
                              q_ref[:, (2 * g + 1) * 128:(2 * g + 2) * 128]], axis=0)

        def sinks(parity):
            top = jnp.full((nq, 1), sink_ref[4 * g + parity], F32)
            bot = jnp.full((nq, 1), sink_ref[4 * g + 2 + parity], F32)
            return jnp.concatenate([top, bot], axis=0)

        o = _sink_attend(q2, k_lo, v_lo, sinks(0), mask) + _sink_attend(q2, k_hi, v_hi, sinks(1), mask)
        o_ref[:, (2 * g) * 128:(2 * g + 1) * 128] = o[:nq].astype(BF16)
        o_ref[:, (2 * g + 1) * 128:(2 * g + 2) * 128] = o[nq:].astype(BF16)


def _attn_ctx_body(sink_ref, q_ref, k_ref, v_ref, o_ref):
    _attend_per_pair(sink_ref, q_ref, o_ref, _head_pair_kv(k_ref[...], v_ref[...]))


def _attn_ctx(sink_l, q, k, v):
    blk = lambda w: pl.BlockSpec((CTX_L, w), lambda b: (b, 0))
    return pl.pallas_call(
        _attn_ctx_body,
        out_shape=jax.ShapeDtypeStruct((T_CTX, ATTN_W), BF16),
        grid=(CTX_B,),
        in_specs=[pl.BlockSpec(memory_space=pltpu.SMEM), blk(ATTN_W), blk(KV_W), blk(KV_W)],
        out_specs=blk(ATTN_W),
        compiler_params=pltpu.CompilerParams(dimension_semantics=("arbitrary",)),
        name="attn_ctx",
    )(sink_l, q, k, v)


def _attn_lat_body(sink_ref, q_ref, kp_ref, kc_ref, kn_ref, vp_ref, vc_ref, vn_ref, ck_ref, cv_ref, o_ref):
    i = pl.program_id(1)
    nblk = LAT_L // LAT_QB
    k = jnp.concatenate([kp_ref[...], kc_ref[...], kn_ref[...], ck_ref[0, 0]], axis=0)
    v = jnp.concatenate([vp_ref[...], vc_ref[...], vn_ref[...], cv_ref[0, 0]], axis=0)
    nloc = LAT_QB + 2 * WINDOW
    nk = nloc + PAST
    r = lax.broadcasted_iota(jnp.int32, (LAT_QB, nk), 0)
    c = lax.broadcasted_iota(jnp.int32, (LAT_QB, nk), 1)
    local = (c - r >= 0) & (c - r <= 2 * WINDOW)
    local = local & ((c >= WINDOW) | (i > 0)) & ((c < WINDOW + LAT_QB) | (i < nblk - 1))
    mask = local | (c >= nloc)
    _attend_stacked(sink_ref, q_ref, o_ref, _head_pair_kv(k, v), mask)


def _attn_lat(sink_l, q, k, v, cache_k, cache_v, layer):
    nblk = LAT_L // LAT_QB
    per = LAT_QB // WINDOW
    nhalo = LAT_L // WINDOW
    base = T_CTX // LAT_QB
    hbase = T_CTX // WINDOW
    cur = lambda b, i: (base + b * nblk + i, 0)
    prev = lambda b, i: (hbase + b * nhalo + jnp.maximum(i * per - 1, 0), 0)
    nxt = lambda b, i: (hbase + b * nhalo + jnp.minimum((i + 1) * per, nhalo - 1), 0)
    kvb = lambda rows, f: pl.BlockSpec((rows, KV_W), f)
    cache = pl.BlockSpec((1, 1, PAST, KV_W), lambda b, i: (b, layer, 0, 0))
    return pl.pallas_call(
        _attn_lat_body,
        out_shape=jax.ShapeDtypeStruct((T_LAT, ATTN_W), BF16),
        grid=(LAT_B, nblk),
        in_specs=[pl.BlockSpec(memory_space=pltpu.SMEM), pl.BlockSpec((LAT_QB, ATTN_W), cur),
                  kvb(WINDOW, prev), kvb(LAT_QB, cur), kvb(WINDOW, nxt),
                  kvb(WINDOW, prev), kvb(LAT_QB, cur), kvb(WINDOW, nxt), cache, cache],
        out_specs=pl.BlockSpec((LAT_QB, ATTN_W), lambda b, i: (b * nblk + i, 0)),
        compiler_params=pltpu.CompilerParams(dimension_semantics=("arbitrary", "arbitrary")),
        name="attn_lat",
    )(sink_l, q, k, k, k, v, v, v, cache_k, cache_v)


def _conv_body(up_ref, uc_ref, un_ref, w_ref, b_ref, g_ref, beta_ref, o_ref, pad_ref):
    i = pl.program_id(0)
    n_ctx = T_CTX // TM_CONV
    per_seq = LAT_L // TM_CONV
    j = (i - n_ctx) % per_seq
    has_prev = (i >= n_ctx) & (j > 0)
    has_next = (i >= n_ctx) & (j < per_seq - 1)
    zero = jnp.zeros((CONV_HALO, CONV_W), F32)
    pad_ref[0, 0:CONV_HALO, :] = jnp.where(has_prev, up_ref[...].astype(F32), zero)
    pad_ref[0, CONV_HALO:CONV_HALO + TM_CONV, :] = uc_ref[...].astype(F32)
    pad_ref[0, CONV_HALO + TM_CONV:, :] = jnp.where(has_next, un_ref[...].astype(F32), zero)
    n_sh = TM_CONV + 2 * CONV_HALO - 8
    for s in range(1, 8):
        pad_ref[s, 0:n_sh, :] = pad_ref[0, s:s + n_sh, :]
    chunk = 64
    for c0 in range(0, TM_CONV, chunk):
        acc = jnp.zeros((chunk, CONV_W), F32) + b_ref[...]
        for t in range(CONV_K):
            src = CONV_HALO - CONV_PAD + t
            row = src - src % 8 + c0
            acc = acc + pad_ref[src % 8, row:row + chunk, :] * w_ref[t:t + 1, :]
        mu = jnp.mean(acc, axis=-1, keepdims=True)
        cen = acc - mu
        var = jnp.mean(cen * cen, axis=-1, keepdims=True)
        y = cen * lax.rsqrt(var + EPS) * g_ref[...] + beta_ref[...]
        o_ref[c0:c0 + chunk, :] = (y * _sigmoid(y)).astype(BF16)


def _conv(u, dw_w, dw_b, ln_g, ln_b):
    per = TM_CONV // CONV_HALO
    last = T_ALL // CONV_HALO - 1
    full = lambda shape: pl.BlockSpec(shape, lambda i: (0,) * len(shape))
    return pl.pallas_call(
        _conv_body,
        out_shape=jax.ShapeDtypeStruct((T_ALL, CONV_W), BF16),
        grid=(T_ALL // TM_CONV,),
        in_specs=[
            pl.BlockSpec((CONV_HALO, CONV_W), lambda i: (jnp.maximum(i * per - 1, 0), 0)),
            pl.BlockSpec((TM_CONV, CONV_W), lambda i: (i, 0)),
            pl.BlockSpec((CONV_HALO, CONV_W), lambda i: (jnp.minimum((i + 1) * per, last), 0)),
            full((CONV_K, CONV_W)), full((1, CONV_W)), full((1, CONV_W)), full((1, CONV_W)),
        ],
        out_specs=pl.BlockSpec((TM_CONV, CONV_W), lambda i: (i, 0)),
        scratch_shapes=[pltpu.VMEM((8, TM_CONV + 2 * CONV_HALO, CONV_W), F32)],
        compiler_params=pltpu.CompilerParams(dimension_semantics=("arbitrary",)),
        name="conv_module",
    )(u, u, u, dw_w, dw_b, ln_g, ln_b)


def _row_pick(idx, rows):
    out = jnp.zeros_like(rows[0])
    for e, r in enumerate(rows):
        out = jnp.where(idx == e, r, out)
    return out


def _argmax_rows(rows):
    best = rows[0]
    idx = jnp.zeros(rows[0].shape, jnp.int32)
    for e in range(1, len(rows)):
        upd = rows[e] > best
        idx = jnp.where(upd, e, idx)
        best = jnp.where(upd, rows[e], best)
    return idx


def _route(logits_t, bias):
    scores = _sigmoid(logits_t)
    sel = scores + bias
    s_rows = [scores[e:e + 1, :] for e in range(N_EXP)]
    rows = [sel[e:e + 1, :] for e in range(N_EXP)]
    grp = []
    for g in range(N_GROUPS):
        r = rows[g * EXP_PER_GROUP:(g + 1) * EXP_PER_GROUP]
        best = None
        for a in range(EXP_PER_GROUP):
            for b in range(a + 1, EXP_PER_GROUP):
                s = r[a] + r[b]
                best = s if best is None else jnp.maximum(best, s)
        grp.append(best)
    gidx = _argmax_rows(grp)
    masked = [jnp.where(gidx == e // EXP_PER_GROUP, rows[e], NEG) for e in range(N_EXP)]
    i1 = _argmax_rows(masked)
    i2 = _argmax_rows([jnp.where(i1 == e, -jnp.inf, masked[e]) for e in range(N_EXP)])
    w1 = _row_pick(i1, s_rows)
    w2 = _row_pick(i2, s_rows)
    tot = w1 + w2
    return i1, i2, w1 / tot, w2 / tot


def _merge_body(ac_ref, al_ref, c_ref, g_ref, xc_ref, xl_ref, mod_ref, n2_ref, wap_ref, wcp_ref, wo_ref, wr_ref,
                x1_ref, h_ref, lg_ref):
    step = pl.program_id(0)
    row = _mod_row(step, TM_MERGE)
    g1 = mod_ref[pl.ds(row, 1), 2 * D:3 * D]
    sh2 = mod_ref[pl.ds(row, 1), 3 * D:4 * D]
    sc2 = mod_ref[pl.ds(row, 1), 4 * D:5 * D]
    is_ctx = step < T_CTX // TM_MERGE
    a = jnp.where(is_ctx, ac_ref[...], al_ref[...])
    attn = jnp.dot(a, wap_ref[...], preferred_element_type=F32)
    conv = jnp.dot(c_ref[...], wcp_ref[...], preferred_element_type=F32)
    merged = g_ref[:, 0:D].astype(F32) * attn + g_ref[:, D:2 * D].astype(F32) * conv
    mix = jnp.dot(merged.astype(BF16), wo_ref[...], preferred_element_type=F32)
    x1 = jnp.where(is_ctx, xc_ref[...], xl_ref[...]) + g1 * mix
    x1_ref[...] = x1
    ms = jnp.mean(x1 * x1, axis=-1, keepdims=True)
    h = (x1 * lax.rsqrt(ms + EPS) * n2_ref[...]) * (1.0 + sc2) + sh2
    h_ref[...] = h.astype(BF16)
    lg_ref[...] = lax.dot_general(wr_ref[...], h, (((1,), (1,)), ((), ())),
                                  precision=lax.Precision.HIGHEST, preferred_element_type=F32)


def _merge(attn_ctx, attn_lat, conv_o, gates, x_ctx, x_lat, lat_off, mod_l, n2, wap, wcp, wo, wr_t):
    tm = TM_MERGE
    n_ctx = T_CTX // tm
    full = lambda shape: pl.BlockSpec(shape, lambda i: (0,) * len(shape))
    row = lambda w: pl.BlockSpec((tm, w), lambda i: (i, 0))
    a_ctx = pl.BlockSpec((tm, ATTN_W), lambda i: (jnp.minimum(i, n_ctx - 1), 0))
    a_lat = pl.BlockSpec((tm, ATTN_W), lambda i: (jnp.maximum(i - n_ctx, 0), 0))
    return pl.pallas_call(
        _merge_body,
        out_shape=(
            jax.ShapeDtypeStruct((T_ALL, D), F32),
            jax.ShapeDtypeStruct((T_ALL, D), BF16),
            jax.ShapeDtypeStruct((N_EXP, T_ALL), F32),
        ),
        grid=(T_ALL // tm,),
        in_specs=[a_ctx, a_lat, row(CONV_W), row(2 * D), *_stream_specs(tm, lat_off),
                  full((MOD_ROWS, 6 * D)), full((1, D)),
                  full((ATTN_W, D)), full((CONV_W, D)), full((D, D)), full((N_EXP, D))],
        out_specs=(row(D), row(D), pl.BlockSpec((N_EXP, tm), lambda i: (0, i))),
        compiler_params=pltpu.CompilerParams(dimension_semantics=("arbitrary",), vmem_limit_bytes=VMEM_LIMIT),
        name="merge",
    )(attn_ctx, attn_lat, conv_o, gates, x_ctx, x_lat, mod_l, n2, wap, wcp, wo, wr_t)


def _router_body(lg_ref, rb_ref, lp_ref, tok_ref, cnt_ref):
    tm = TM_ROUTE
    i1, i2, w1, w2 = _route(lg_ref[...], rb_ref[...])

    eio = lax.broadcasted_iota(jnp.int32, (N_EXP, tm), 0)
    oh1 = eio == i1
    oh2 = eio == i2
    oh = (oh1 | oh2).astype(F32)
    nc = ROUTE_CHUNK
    tri = (lax.broadcasted_iota(jnp.int32, (nc, nc), 0) < lax.broadcasted_iota(jnp.int32, (nc, nc), 1)).astype(BF16)
    lane = lax.broadcasted_iota(jnp.int32, (N_EXP, 128), 1)
    esub = lax.broadcasted_iota(jnp.int32, (N_EXP, 1), 0)
    counts = jnp.zeros((N_EXP, 128), F32)
    pos = []
    for s0 in range(0, tm, SORT_TL):
        base = jnp.zeros((N_EXP, 1), F32)
        ranks = []
        for c0 in range(s0, s0 + SORT_TL, nc):
            ohc = oh[:, c0:c0 + nc]
            ranks.append(base + jnp.dot(ohc.astype(BF16), tri, preferred_element_type=F32))
            base = base + jnp.sum(ohc, axis=1, keepdims=True)
        counts = jnp.where(lane == s0 // SORT_TL, base, counts)
        padded = jnp.floor((base + (CHUNK - 1)) * (1.0 / CHUNK)) * CHUNK
        run_start = jnp.zeros((N_EXP, 1), F32)
        for e in range(N_EXP - 1):
            run_start = run_start + jnp.where(esub > e, padded[e:e + 1, :], 0.0)
        pos.append(jnp.concatenate(ranks, axis=1) + run_start)
    pos = jnp.concatenate(pos, axis=1)
    p1 = jnp.sum(jnp.where(oh1, pos, 0.0), axis=0, keepdims=True)
    p2 = jnp.sum(jnp.where(oh2, pos, 0.0), axis=0, keepdims=True)
    cnt_ref[0] = counts

    sub = lax.broadcasted_iota(jnp.int32, (8, tm), 0)
    lp_ref[...] = jnp.where(sub == 0, p1.astype(jnp.int32), jnp.where(sub == 1, p2.astype(jnp.int32), 0))
    rows = jnp.where(sub == 0, p1, jnp.where(sub == 1, p2, jnp.where(sub == 2, w1, jnp.where(sub == 3, w2, 0.0))))
    tok_ref[...] = jnp.transpose(rows)


def _router(logits_t, rb):
    tm = TM_ROUTE
    lanes = lambda rows: pl.BlockSpec((rows, tm), lambda i: (0, i))
    return pl.pallas_call(
        _router_body,
        out_shape=(
            jax.ShapeDtypeStruct((8, T_ALL), jnp.int32),
            jax.ShapeDtypeStruct((T_ALL, 8), F32),
            jax.ShapeDtypeStruct((T_ALL // tm, N_EXP, 128), F32),
        ),
        grid=(T_ALL // tm,),
        in_specs=[lanes(N_EXP), pl.BlockSpec((N_EXP, 1), lambda i: (0, 0))],
        out_specs=(lanes(8), pl.BlockSpec((tm, 8), lambda i: (i, 0)),
                   pl.BlockSpec((1, N_EXP, 128), lambda i: (i, 0, 0))),
        compiler_params=pltpu.CompilerParams(dimension_semantics=("arbitrary",)),
        name="router",
    )(logits_t, rb)


def _sort_tables(cnt):
    per = TM_ROUTE // SORT_TL
    c = jnp.transpose(cnt[:, :, :per], (0, 2, 1)).reshape(N_SORT, N_EXP).astype(jnp.int32)
    pc = (c + (CHUNK - 1)) // CHUNK * CHUNK
    loff = jnp.cumsum(pc, axis=1) - pc
    used = jnp.sum(pc, axis=1)
    seg = jnp.sum(pc, axis=0)
    ends = jnp.cumsum(seg)
    starts = ends - seg
    gstart = starts[None, :] + jnp.cumsum(pc, axis=0) - pc
    j = jnp.arange(N_CHUNK, dtype=jnp.int32)[None, :, None]
    lo = (loff // CHUNK)[:, None, :]
    in_run = (j >= lo) & (j < lo + (pc // CHUNK)[:, None, :])
    dst = jnp.sum(jnp.where(in_run, (gstart // CHUNK)[:, None, :] + j - lo, 0), axis=-1)
    return (dst.reshape(-1).astype(jnp.int32), (used // CHUNK).astype(jnp.int32),
            starts.astype(jnp.int32), ends.astype(jnp.int32))


def _run_copies(dst_ref, used_ref, tile, make_copy, act):
    def body(j, carry):
        act(make_copy(j, dst_ref[tile * N_CHUNK + j]))
        return carry

    lax.fori_loop(0, used_ref[tile], body, 0)


def _sort_body(dst_ref, used_ref, tail_ref, lp_ref, h_ref, xs_hbm, xl_ref, sem):
    i = pl.program_id(0)
    slot = i % 2
    last = pl.num_programs(0) - 1
    r = lax.broadcasted_iota(jnp.int32, (CAP, SORT_TL), 0)
    place = ((r == lp_ref[0:1, :]) | (r == lp_ref[1:2, :])).astype(BF16)
    xl_ref[slot] = jnp.dot(place, h_ref[...], preferred_element_type=F32).astype(BF16)

    def copy_from(buf):
        def make(src, dst):
            return pltpu.make_async_copy(xl_ref.at[buf, pl.ds(src * CHUNK, CHUNK), :],
                                         xs_hbm.at[pl.ds(dst * CHUNK, CHUNK), :], sem.at[buf])
        return make

    def tail_copies(act):
        def body(j, carry):
            act(copy_from(slot)(N_CHUNK - 1, tail_ref[0] + j))
            return carry
        lax.fori_loop(0, tail_ref[1], body, 0)

    _run_copies(dst_ref, used_ref,i, copy_from(slot), lambda c: c.start())

    @pl.when(i == last)
    def _():
        tail_copies(lambda c: c.start())

    @pl.when(i > 0)
    def _():
        _run_copies(dst_ref, used_ref,i - 1, copy_from(1 - slot), lambda c: c.wait())

    @pl.when(i == last)
    def _():
        _run_copies(dst_ref, used_ref,i, copy_from(slot), lambda c: c.wait())
        tail_copies(lambda c: c.wait())


def _sort_scatter(dst, used, tail, lpos8, h):
    return pl.pallas_call(
        _sort_body,
        out_shape=jax.ShapeDtypeStruct((XS_ROWS, D), BF16),
        grid_spec=pltpu.PrefetchScalarGridSpec(
            num_scalar_prefetch=3,
            grid=(N_SORT,),
            in_specs=[pl.BlockSpec((8, SORT_TL), lambda i, *_: (0, i)),
                      pl.BlockSpec((SORT_TL, D), lambda i, *_: (i, 0))],
            out_specs=pl.BlockSpec(memory_space=pl.ANY),
            scratch_shapes=[pltpu.VMEM((2, CAP, D), BF16), pltpu.SemaphoreType.DMA((2,))],
        ),
        compiler_params=pltpu.CompilerParams(dimension_semantics=("arbitrary",), vmem_limit_bytes=VMEM_LIMIT),
        name="sort_scatter",
    )(dst, used, tail, lpos8, h)


def _ffn_body(tile_ref, exp_ref, lo_ref, hi_ref, first_ref, mode_ref, fresh_ref,
              x_ref, wg_ref, wu_ref, wd_ref, o_ref, wg_b, wu_b, wd_b):
    u = pl.program_id(0)

    @pl.when(mode_ref[u] == 2)
    def _():
        o_ref[...] = jnp.zeros_like(o_ref)

    @pl.when(fresh_ref[u] == 1)
    def _():
        wg_b[...] = wg_ref[0, 0].astype(BF16)
        wu_b[...] = wu_ref[0, 0].astype(BF16)
        wd_b[...] = wd_ref[0, 0].astype(BF16)

    @pl.when(mode_ref[u] == 1)
    def _():
        parts = []
        for r0 in range(0, TM_FFN, FFN_SUB):
            x = x_ref[r0:r0 + FFN_SUB, :]
            g = jnp.dot(x, wg_b[...], preferred_element_type=F32)
            up = jnp.dot(x, wu_b[...], preferred_element_type=F32)
            rows = r0 + lax.broadcasted_iota(jnp.int32, (FFN_SUB, 1), 0)
            mine = (rows >= lo_ref[u]) & (rows < hi_ref[u])
            hid = jnp.where(mine, g * _sigmoid(g) * up, 0.0).astype(BF16)
            parts.append(jnp.dot(hid, wd_b[...], preferred_element_type=F32).astype(BF16))
        y = jnp.concatenate(parts, axis=0)

        @pl.when(first_ref[u] == 1)
        def _():
            o_ref[...] = y

        @pl.when(first_ref[u] == 0)
        def _():
            o_ref[...] += y


def _ffn(units, xs, wg, wu, wd, layer):
    xmap = lambda u, tile, *_: (tile[u], 0)
    wmap = lambda u, tile, exp, *_: (layer, exp[u], 0, 0)
    return pl.pallas_call(
        _ffn_body,
        out_shape=jax.ShapeDtypeStruct((XS_ROWS, D), BF16),
        grid_spec=pltpu.PrefetchScalarGridSpec(
            num_scalar_prefetch=7,
            grid=(N_UNITS,),
            in_specs=[pl.BlockSpec((TM_FFN, D), xmap),
                      pl.BlockSpec((1, 1, D, D_EXP), wmap), pl.BlockSpec((1, 1, D, D_EXP), wmap),
                      pl.BlockSpec((1, 1, D_EXP, D), wmap)],
            out_specs=pl.BlockSpec((TM_FFN, D), xmap),
            scratch_shapes=[pltpu.VMEM((D, D_EXP), BF16), pltpu.VMEM((D, D_EXP), BF16), pltpu.VMEM((D_EXP, D), BF16)],
        ),
        compiler_params=pltpu.CompilerParams(dimension_semantics=("arbitrary",), vmem_limit_bytes=VMEM_LIMIT),
        name="expert_ffn",
    )(*units, xs, wg, wu, wd)


def _ffn_units(starts, ends):
    total_rows = ends[-1]
    t0 = jnp.arange(N_FFN_TILES, dtype=jnp.int32) * TM_FFN
    t1 = jnp.minimum(t0 + TM_FFN, total_rows) - 1
    e_first = jnp.sum(ends[None, :] <= t0[:, None], axis=1).astype(jnp.int32)
    e_last = jnp.sum(ends[None, :] <= t1[:, None], axis=1).astype(jnp.int32)
    n_per = jnp.where(t0 < total_rows, e_last - e_first + 1, 0)
    u_end = jnp.cumsum(n_per)
    u_start = u_end - n_per
    total = u_end[-1]
    u = jnp.arange(N_UNITS, dtype=jnp.int32)
    uc = jnp.minimum(u, total - 1)
    tile = jnp.sum(u_end[None, :] <= uc[:, None], axis=1).astype(jnp.int32)
    exp = e_first[tile] + (uc - u_start[tile])
    lo = jnp.clip(starts[exp] - tile * TM_FFN, 0, TM_FFN)
    hi = jnp.clip(ends[exp] - tile * TM_FFN, 0, TM_FFN)
    first = ((uc == u_start[tile]) & (u < total)).astype(jnp.int32)
    exp = exp.astype(jnp.int32)
    fresh = jnp.concatenate([jnp.ones((1,), jnp.int32), (exp[1:] != exp[:-1]).astype(jnp.int32)])
    spare = jnp.sum(t0 < total_rows).astype(jnp.int32) + (u - total)
    mode = jnp.where(u < total, 1, jnp.where(spare < N_FFN_TILES, 2, 0)).astype(jnp.int32)
    tile = jnp.where(u < total, tile, jnp.minimum(spare, N_FFN_TILES - 1)).astype(jnp.int32)
    return tile, exp, lo.astype(jnp.int32), hi.astype(jnp.int32), first, mode, fresh


def _combine_body(dst_ref, used_ref, tok_ref, x1_ref, mod_ref, ys_hbm, o_ref, yl_ref, sem):
    i = pl.program_id(0)
    slot = i % 2
    last = pl.num_programs(0) - 1

    def copy_into(buf):
        def make(loc, glob):
            return pltpu.make_async_copy(ys_hbm.at[pl.ds(glob * CHUNK, CHUNK), :],
                                         yl_ref.at[buf, pl.ds(loc * CHUNK, CHUNK), :], sem.at[buf])
        return make

    @pl.when(i == 0)
    def _():
        _run_copies(dst_ref, used_ref,i, copy_into(slot), lambda c: c.start())

    @pl.when(i < last)
    def _():
        _run_copies(dst_ref, used_ref,i + 1, copy_into(1 - slot), lambda c: c.start())

    row = _mod_row(i, SORT_TL)
    g2 = mod_ref[pl.ds(row, 1), 5 * D:6 * D]
    lane = lax.broadcasted_iota(jnp.int32, (SORT_TL, CAP), 1)
    slot1 = tok_ref[:, 0:1].astype(jnp.int32)
    slot2 = tok_ref[:, 1:2].astype(jnp.int32)
    comb = (jnp.where(lane == slot1, tok_ref[:, 2:3], 0.0)
            + jnp.where(lane == slot2, tok_ref[:, 3:4], 0.0)).astype(BF16)

    _run_copies(dst_ref, used_ref,i, copy_into(slot), lambda c: c.wait())

    def clear(j, carry):
        yl_ref[slot, pl.ds(pl.multiple_of(j * CHUNK, CHUNK), CHUNK), :] = jnp.zeros((CHUNK, D), BF16)
        return carry

    lax.fori_loop(used_ref[i], N_CHUNK, clear, 0)
    y = jnp.dot(comb, yl_ref[slot], preferred_element_type=F32)
    o_ref[...] = x1_ref[...] + g2 * y


def _combine(dst, used, tok8, x1, mod_l, ys):
    tok = lambda w: pl.BlockSpec((SORT_TL, w), lambda i, *_: (i, 0))
    return pl.pallas_call(
        _combine_body,
        out_shape=jax.ShapeDtypeStruct((T_ALL, D), F32),
        grid_spec=pltpu.PrefetchScalarGridSpec(
            num_scalar_prefetch=2,
            grid=(N_SORT,),
            in_specs=[tok(8), tok(D),
                      pl.BlockSpec((MOD_ROWS, 6 * D), lambda i, *_: (0, 0)),
                      pl.BlockSpec(memory_space=pl.ANY)],
            out_specs=tok(D),
            scratch_shapes=[pltpu.VMEM((2, CAP, D), BF16), pltpu.SemaphoreType.DMA((2,))],
        ),
        compiler_params=pltpu.CompilerParams(dimension_semantics=("arbitrary",), vmem_limit_bytes=VMEM_LIMIT),
        name="combine",
    )(dst, used, tok8, x1, mod_l, ys)


def _final_norm_body(x_ref, g_ref, o_ref):
    x = x_ref[...]
    ms = jnp.mean(x * x, axis=-1, keepdims=True)
    o_ref[...] = x * lax.rsqrt(ms + EPS) * g_ref[...]


def _final_norm(x, gain, row0, rows):
    tm = 512
    first = row0 // tm
    return pl.pallas_call(
        _final_norm_body,
        out_shape=jax.ShapeDtypeStruct((rows, D), F32),
        grid=(rows // tm,),
        in_specs=[pl.BlockSpec((tm, D), lambda i: (first + i, 0)), pl.BlockSpec((1, D), lambda i: (0, 0))],
        out_specs=pl.BlockSpec((tm, D), lambda i: (i, 0)),
        name="final_norm",
    )(x, gain)


def _rope_tables():
    pos = jnp.arange(LAT_L)
    rowp = (pos // GRID_W).astype(F32)
    colp = (pos % GRID_W).astype(F32)
    pairs = HD // 4
    inv = ROPE_BASE ** (-jnp.arange(pairs, dtype=F32) / pairs)
    ang = jnp.concatenate([rowp[:, None] * inv] * 2 + [colp[:, None] * inv] * 2, axis=-1)
    cos = jnp.tile(jnp.cos(ang), (1, N_HEADS))
    sin = jnp.tile(jnp.sin(ang), (1, N_HEADS))
    cos = jnp.concatenate([jnp.ones((TM_IN, ATTN_W), F32), cos], axis=0)
    sin = jnp.concatenate([jnp.zeros((TM_IN, ATTN_W), F32), sin], axis=0)
    return cos, sin


def _head_matrices():
    i = jnp.arange(ATTN_W)
    bd = (i[:, None] // HD == i[None, :] // HD).astype(BF16)
    half = HD // 4
    j = i[None, :]
    src = i[:, None]
    first = (j % (2 * half)) < half
    rm = jnp.where(first & (src == j + half), -1.0, 0.0) + jnp.where(~first & (src == j - half), 1.0, 0.0)
    return bd, rm.astype(BF16)


def kernel(x_prompt, x_sample, cache_k, cache_v, c, c_ctx, w_ada, b_ada, norm1, norm2, w_in, q_norm, k_norm,
           sink, w_attn_proj, dw_w, dw_b, cln_g, cln_b, w_conv_proj, w_out, w_router, router_bias,
           w_e_gate, w_e_up, w_e_down, final_norm):
    x_ctx, x_lat = x_prompt.reshape(T_CTX, D), x_sample.reshape(T_LAT, D)
    cond = jnp.concatenate([c_ctx[None, :], c, jnp.zeros((MOD_ROWS - 1 - LAT_B, D), F32)], axis=0)
    mod = _ada_table(cond, w_ada, b_ada)
    cos_t, sin_t = _rope_tables()
    bd, rm = _head_matrices()
    w_in_b = w_in.astype(BF16)
    wap_b = w_attn_proj.astype(BF16)
    wcp_b = w_conv_proj.astype(BF16)
    wo_b = w_out.astype(BF16)
    wr_t = w_router.T
    rb = router_bias.reshape(N_EXP, 1)
    ck = cache_k.reshape(LAT_B, DEPTH, PAST, KV_W)
    cv = cache_v.reshape(LAT_B, DEPTH, PAST, KV_W)

    new_k, new_v = [], []
    for l in range(DEPTH):
        merged = l > 0
        q, k, v, u, gates = _inproj(x_ctx, x_lat, T_CTX // TM_IN if merged else 0, mod[l], norm1[l][None, :], w_in_b[l],
                                    jnp.tile(q_norm[l], N_HEADS)[None, :], jnp.tile(k_norm[l], N_KV)[None, :],
                                    cos_t, sin_t, bd, rm)
        new_k.append(k[:T_CTX].reshape(CTX_B, CTX_L, N_KV, HD))
        new_v.append(v[:T_CTX].reshape(CTX_B, CTX_L, N_KV, HD))
        o_ctx = _attn_ctx(sink[l], q, k, v)
        o_lat = _attn_lat(sink[l], q, k, v, ck, cv, l)
        conv_o = _conv(u, dw_w[l], dw_b[l][None, :], cln_g[l][None, :], cln_b[l][None, :])
        x1, h, logits_t = _merge(o_ctx, o_lat, conv_o, gates, x_ctx, x_lat, T_CTX // TM_MERGE if merged else 0,
                                 mod[l], norm2[l][None, :], wap_b[l], wcp_b[l], wo_b[l], wr_t)
        lpos, tok, cnt = _router(logits_t, rb)
        dst, used, starts, ends = _sort_tables(cnt)
        rows_ch = ends[-1] // CHUNK
        tail = jnp.stack([rows_ch, XS_ROWS // CHUNK - rows_ch]).astype(jnp.int32)
        xs = _sort_scatter(dst, used, tail, lpos, h)
        ys = _ffn(_ffn_units(starts, ends), xs, w_e_gate, w_e_up, w_e_down, l)
        x_ctx = x_lat = _combine(dst, used, tok, x1, mod[l], ys)

    gain = final_norm[None, :]
    y_prompt = _final_norm(x_ctx, gain, 0, T_CTX).reshape(CTX_B, CTX_L, D)
    y_sample = _final_norm(x_lat, gain, T_CTX, T_LAT).reshape(LAT_B, LAT_L, D)
    return y_prompt, y_sample, jnp.stack(new_k, axis=1), jnp.stack(new_v, axis=1)
```

```python
import jax
import jax.numpy as jnp
from jax import lax
from jax.experimental import pallas as pl
from jax.experimental.pallas import tpu as pltpu

F32 = jnp.float32
BF16 = jnp.bfloat16

D = 1024
DEPTH = 4
CTX_B, CTX_L = 32, 256
LAT_B, LAT_L = 8, 1024
PAST = 512
T_CTX = CTX_B * CTX_L
T_LAT = LAT_B * LAT_L
T_ALL = T_CTX + T_LAT
GRID_W = 64
HD = 64
N_HEADS = 8
N_KV = 2
ATTN_W = N_HEADS * HD
KV_W = N_KV * HD
WINDOW = 128
LAT_QB = 256
CONV_W = 512
CONV_K = 31
CONV_PAD = CONV_K // 2
N_EXP = 16
N_GROUPS = 4
EXP_PER_GROUP = N_EXP // N_GROUPS
D_EXP = 512
IN_W = ATTN_W + 2 * KV_W + 2 * CONV_W + 2 * D
OFF_K = ATTN_W
OFF_V = ATTN_W + KV_W
OFF_A = ATTN_W + 2 * KV_W
OFF_B = OFF_A + CONV_W
OFF_G = OFF_A + 2 * CONV_W
EPS = 1e-6
NEG = -1e30
ROPE_BASE = 10000.0
MOD_ROWS = 16

TM_IN = 512
TM_MERGE = 512
TM_ROUTE = 2048
ROUTE_CHUNK = 256
TM_CONV = 256
CONV_HALO = 16
SORT_TL = 512
N_SORT = T_ALL // SORT_TL
CHUNK = 16
CAP = 2 * SORT_TL + 2 * 128
N_CHUNK = CAP // CHUNK
TM_FFN = 512
FFN_SUB = 512
XS_ROWS = 2 * T_ALL + N_SORT * N_EXP * (CHUNK - 1)
N_FFN_TILES = XS_ROWS // TM_FFN
N_UNITS = N_FFN_TILES + N_EXP - 1

VMEM_LIMIT = 56 * 1024 * 1024


def _sigmoid(x):
    return 1.0 / (1.0 + jnp.exp(-x))


def _mod_row(tile, tm):
    start = tile * tm
    return jnp.where(start < T_CTX, 0, 1 + (start - T_CTX) // LAT_L)


def _stream_specs(tm, lat_off):
    n_ctx = T_CTX // tm
    return (pl.BlockSpec((tm, D), lambda i, *_: (jnp.minimum(i, n_ctx - 1), 0)),
            pl.BlockSpec((tm, D), lambda i, *_: (jnp.maximum(i - n_ctx, 0) + lat_off, 0)))


def _ada_body(cond_ref, w_ref, b_ref, o_ref):
    c = cond_ref[...]
    s = (c * _sigmoid(c)).astype(BF16)
    o_ref[0] = jnp.dot(s, w_ref[0].astype(BF16), preferred_element_type=F32) + b_ref[0]


def _ada_table(cond, w_ada, b_ada):
    nj = 6 * D // 1024
    return pl.pallas_call(
        _ada_body,
        out_shape=jax.ShapeDtypeStruct((DEPTH, MOD_ROWS, 6 * D), F32),
        grid=(DEPTH, nj),
        in_specs=[
            pl.BlockSpec((MOD_ROWS, D), lambda l, j: (0, 0)),
            pl.BlockSpec((1, D, 1024), lambda l, j: (l, 0, j)),
            pl.BlockSpec((1, 1, 1024), lambda l, j: (l, 0, j)),
        ],
        out_specs=pl.BlockSpec((1, MOD_ROWS, 1024), lambda l, j: (l, 0, j)),
        name="ada_table",
    )(cond, w_ada, b_ada.reshape(DEPTH, 1, 6 * D))


def _head_norm_rope(y, gain, cos, sin, bd, rm):
    ss = jnp.dot((y * y).astype(BF16), bd, preferred_element_type=F32)
    yn = y * lax.rsqrt(ss * (1.0 / HD) + EPS) * gain
    rot = jnp.dot(yn.astype(BF16), rm, preferred_element_type=F32)
    return yn * cos + rot * sin


def _inproj_body(xc_ref, xl_ref, mod_ref, n1_ref, w_ref, qg_ref, kg_ref, cos_ref, sin_ref, bd_ref, rm_ref,
                 q_ref, k_ref, v_ref, u_ref, g_ref):
    step = pl.program_id(0)
    row = _mod_row(step, TM_IN)
    sh = mod_ref[pl.ds(row, 1), 0:D]
    sc = mod_ref[pl.ds(row, 1), D:2 * D]
    x = jnp.where(step < T_CTX // TM_IN, xc_ref[...], xl_ref[...])
    ms = jnp.mean(x * x, axis=-1, keepdims=True)
    h = ((x * lax.rsqrt(ms + EPS) * n1_ref[...]) * (1.0 + sc) + sh).astype(BF16)

    def proj(lo, hi):
        return jnp.dot(h, w_ref[:, lo:hi], preferred_element_type=F32)

    cos = cos_ref[...]
    sin = sin_ref[...]
    q = _head_norm_rope(proj(0, OFF_K), qg_ref[...], cos, sin, bd_ref[...], rm_ref[...])
    q_ref[...] = (q * (HD ** -0.5)).astype(BF16)
    k = _head_norm_rope(proj(OFF_K, OFF_V), kg_ref[...], cos[:, :KV_W], sin[:, :KV_W],
                        bd_ref[:KV_W, :KV_W], rm_ref[:KV_W, :KV_W])
    k_ref[...] = k
    v_ref[...] = proj(OFF_V, OFF_A)
    a = proj(OFF_A, OFF_B)
    b = proj(OFF_B, OFF_G)
    u_ref[...] = (a * _sigmoid(b)).astype(BF16)
    for j in range(2):
        g = proj(OFF_G + j * D, OFF_G + (j + 1) * D)
        g_ref[:, j * D:(j + 1) * D] = _sigmoid(g).astype(BF16)


def _inproj(x_ctx, x_lat, lat_off, mod_l, n1, w_in, qg, kg, cos_t, sin_t, bd, rm):
    n_ctx_tiles = T_CTX // TM_IN
    tiles_per_seq = LAT_L // TM_IN

    def tab(i):
        return (jnp.where(i < n_ctx_tiles, 0, 1 + (i - n_ctx_tiles) % tiles_per_seq), 0)

    full = lambda shape: pl.BlockSpec(shape, lambda i: (0,) * len(shape))
    row = lambda w: pl.BlockSpec((TM_IN, w), lambda i: (i, 0))
    return pl.pallas_call(
        _inproj_body,
        out_shape=(
            jax.ShapeDtypeStruct((T_ALL, ATTN_W), BF16),
            jax.ShapeDtypeStruct((T_ALL, KV_W), F32),
            jax.ShapeDtypeStruct((T_ALL, KV_W), F32),
            jax.ShapeDtypeStruct((T_ALL, CONV_W), BF16),
            jax.ShapeDtypeStruct((T_ALL, 2 * D), BF16),
        ),
        grid=(T_ALL // TM_IN,),
        in_specs=[
            *_stream_specs(TM_IN, lat_off), full((MOD_ROWS, 6 * D)), full((1, D)), full((D, IN_W)),
            full((1, ATTN_W)), full((1, KV_W)),
            pl.BlockSpec((TM_IN, ATTN_W), tab), pl.BlockSpec((TM_IN, ATTN_W), tab),
            full((ATTN_W, ATTN_W)), full((ATTN_W, ATTN_W)),
        ],
        out_specs=(row(ATTN_W), row(KV_W), row(KV_W), row(CONV_W), row(2 * D)),
        compiler_params=pltpu.CompilerParams(dimension_semantics=("arbitrary",), vmem_limit_bytes=VMEM_LIMIT),
        name="inproj",
    )(x_ctx, x_lat, mod_l, n1, w_in, qg, kg, cos_t, sin_t, bd, rm)


def _head_pair_kv(k, v):
    lane = lax.broadcasted_iota(jnp.int32, k.shape, 1)
    low = lane < HD
    zero = jnp.zeros_like(k)
    k_sw = pltpu.roll(k, HD, 1)
    v_sw = pltpu.roll(v, HD, 1)
    g0 = (jnp.where(low, k, zero), jnp.where(low, v, zero), jnp.where(low, zero, k_sw), jnp.where(low, zero, v_sw))
    g1 = (jnp.where(low, k_sw, zero), jnp.where(low, v_sw, zero), jnp.where(low, zero, k), jnp.where(low, zero, v))
    return tuple(tuple(t.astype(BF16) for t in g) for g in (g0, g1))


def _sink_attend(qp, kk, vv, sink, mask):
    s = lax.dot_general(qp, kk, (((1,), (1,)), ((), ())), preferred_element_type=F32)
    if mask is not None:
        s = jnp.where(mask, s, NEG)
    m = jnp.maximum(jnp.max(s, axis=-1, keepdims=True), sink)
    p = jnp.exp(s - m)
    den = jnp.sum(p, axis=-1, keepdims=True) + jnp.exp(sink - m)
    return jnp.dot(p.astype(BF16), vv, preferred_element_type=F32) / den


def _attend_per_pair(sink_ref, q_ref, o_ref, kv):
    for pair in range(N_HEADS // 2):
        k_lo, v_lo, k_hi, v_hi = kv[pair // 2]
        qp = q_ref[:, pair * 128:(pair + 1) * 128]
        o = (_sink_attend(qp, k_lo, v_lo, sink_ref[2 * pair], None)
             + _sink_attend(qp, k_hi, v_hi, sink_ref[2 * pair + 1], None))
        o_ref[:, pair * 128:(pair + 1) * 128] = o.astype(BF16)


def _attend_stacked(sink_ref, q_ref, o_ref, kv, mask):
    nq = q_ref.shape[0]
    mask = jnp.concatenate([mask, mask], axis=0)
    for g in range(N_KV):
        k_lo, v_lo, k_hi, v_hi = kv[g]
        q2 = jnp.concatenate([q_ref[:, (2 * g) * 128:(2 * g + 1) * 128],
                              q_ref[:, (2 * g + 1) * 128:(2 * g + 2) * 128]], axis=0)

        def sinks(parity):
            top = jnp.full((nq, 1), sink_ref[4 * g + parity], F32)
            bot = jnp.full((nq, 1), sink_ref[4 * g + 2 + parity], F32)
            return jnp.concatenate([top, bot], axis=0)

        o = _sink_attend(q2, k_lo, v_lo, sinks(0), mask) + _sink_attend(q2, k_hi, v_hi, sinks(1), mask)
        o_ref[:, (2 * g) * 128:(2 * g + 1) * 128] = o[:nq].astype(BF16)
        o_ref[:, (2 * g + 1) * 128:(2 * g + 2) * 128] = o[nq:].astype(BF16)


def _attn_ctx_body(sink_ref, q_ref, k_ref, v_ref, o_ref):
    _attend_per_pair(sink_ref, q_ref, o_ref, _head_pair_kv(k_ref[...], v_ref[...]))


def _attn_ctx(sink_l, q, k, v):
    blk = lambda w: pl.BlockSpec((CTX_L, w), lambda b: (b, 0))
    return pl.pallas_call(
        _attn_ctx_body,
        out_shape=jax.ShapeDtypeStruct((T_CTX, ATTN_W), BF16),
        grid=(CTX_B,),
        in_specs=[pl.BlockSpec(memory_space=pltpu.SMEM), blk(ATTN_W), blk(KV_W), blk(KV_W)],
        out_specs=blk(ATTN_W),
        compiler_params=pltpu.CompilerParams(dimension_semantics=("arbitrary",)),
        name="attn_ctx",
    )(sink_l, q, k, v)


def _attn_lat_body(sink_ref, q_ref, kp_ref, kc_ref, kn_ref, vp_ref, vc_ref, vn_ref, ck_ref, cv_ref, o_ref):
    i = pl.program_id(1)
    nblk = LAT_L // LAT_QB
    k = jnp.concatenate([kp_ref[...], kc_ref[...], kn_ref[...], ck_ref[0, 0]], axis=0)
    v = jnp.concatenate([vp_ref[...], vc_ref[...], vn_ref[...], cv_ref[0, 0]], axis=0)
    nloc = LAT_QB + 2 * WINDOW
    nk = nloc + PAST
    r = lax.broadcasted_iota(jnp.int32, (LAT_QB, nk), 0)
    c = lax.broadcasted_iota(jnp.int32, (LAT_QB, nk), 1)
    local = (c - r >= 0) & (c - r <= 2 * WINDOW)
    local = local & ((c >= WINDOW) | (i > 0)) & ((c < WINDOW + LAT_QB) | (i < nblk - 1))
    mask = local | (c >= nloc)
    _attend_stacked(sink_ref, q_ref, o_ref, _head_pair_kv(k, v), mask)


def _attn_lat(sink_l, q, k, v, cache_k, cache_v, layer):
    nblk = LAT_L // LAT_QB
    per = LAT_QB // WINDOW
    nhalo = LAT_L // WINDOW
    base = T_CTX // LAT_QB
    hbase = T_CTX // WINDOW
    cur = lambda b, i: (base + b * nblk + i, 0)
    prev = lambda b, i: (hbase + b * nhalo + jnp.maximum(i * per - 1, 0), 0)
    nxt = lambda b, i: (hbase + b * nhalo + jnp.minimum((i + 1) * per, nhalo - 1), 0)
    kvb = lambda rows, f: pl.BlockSpec((rows, KV_W), f)
    cache = pl.BlockSpec((1, 1, PAST, KV_W), lambda b, i: (b, layer, 0, 0))
    return pl.pallas_call(
        _attn_lat_body,
        out_shape=jax.ShapeDtypeStruct((T_LAT, ATTN_W), BF16),
        grid=(LAT_B, nblk),
        in_specs=[pl.BlockSpec(memory_space=pltpu.SMEM), pl.BlockSpec((LAT_QB, ATTN_W), cur),
                  kvb(WINDOW, prev), kvb(LAT_QB, cur), kvb(WINDOW, nxt),
                  kvb(WINDOW, prev), kvb(LAT_QB, cur), kvb(WINDOW, nxt), cache, cache],
        out_specs=pl.BlockSpec((LAT_QB, ATTN_W), lambda b, i: (b * nblk + i, 0)),
        compiler_params=pltpu.CompilerParams(dimension_semantics=("arbitrary", "arbitrary")),
        name="attn_lat",
    )(sink_l, q, k, k, k, v, v, v, cache_k, cache_v)


def _conv_body(up_ref, uc_ref, un_ref, w_ref, b_ref, g_ref, beta_ref, o_ref, pad_ref):
    i = pl.program_id(0)
    n_ctx = T_CTX // TM_CONV
    per_seq = LAT_L // TM_CONV
    j = (i - n_ctx) % per_seq
    has_prev = (i >= n_ctx) & (j > 0)
    has_next = (i >= n_ctx) & (j < per_seq - 1)
    zero = jnp.zeros((CONV_HALO, CONV_W), F32)
    pad_ref[0, 0:CONV_HALO, :] = jnp.where(has_prev, up_ref[...].astype(F32), zero)
    pad_ref[0, CONV_HALO:CONV_HALO + TM_CONV, :] = uc_ref[...].astype(F32)
    pad_ref[0, CONV_HALO + TM_CONV:, :] = jnp.where(has_next, un_ref[...].astype(F32), zero)
    n_sh = TM_CONV + 2 * CONV_HALO - 8
    for s in range(1, 8):
        pad_ref[s, 0:n_sh, :] = pad_ref[0, s:s + n_sh, :]
    chunk = 64
    for c0 in range(0, TM_CONV, chunk):
        acc = jnp.zeros((chunk, CONV_W), F32) + b_ref[...]
        for t in range(CONV_K):
            src = CONV_HALO - CONV_PAD + t
            row = src - src % 8 + c0
            acc = acc + pad_ref[src % 8, row:row + chunk, :] * w_ref[t:t + 1, :]
        mu = jnp.mean(acc, axis=-1, keepdims=True)
        cen = acc - mu
        var = jnp.mean(cen * cen, axis=-1, keepdims=True)
        y = cen * lax.rsqrt(var + EPS) * g_ref[...] + beta_ref[...]
        o_ref[c0:c0 + chunk, :] = (y * _sigmoid(y)).astype(BF16)


def _conv(u, dw_w, dw_b, ln_g, ln_b):
    per = TM_CONV // CONV_HALO
    last = T_ALL // CONV_HALO - 1
    full = lambda shape: pl.BlockSpec(shape, lambda i: (0,) * len(shape))
    return pl.pallas_call(
        _conv_body,
        out_shape=jax.ShapeDtypeStruct((T_ALL, CONV_W), BF16),
        grid=(T_ALL // TM_CONV,),
        in_specs=[
            pl.BlockSpec((CONV_HALO, CONV_W), lambda i: (jnp.maximum(i * per - 1, 0), 0)),
            pl.BlockSpec((TM_CONV, CONV_W), lambda i: (i, 0)),
            pl.BlockSpec((CONV_HALO, CONV_W), lambda i: (jnp.minimum((i + 1) * per, last), 0)),
            full((CONV_K, CONV_W)), full((1, CONV_W)), full((1, CONV_W)), full((1, CONV_W)),
        ],
        out_specs=pl.BlockSpec((TM_CONV, CONV_W), lambda i: (i, 0)),
        scratch_shapes=[pltpu.VMEM((8, TM_CONV + 2 * CONV_HALO, CONV_W), F32)],
        compiler_params=pltpu.CompilerParams(dimension_semantics=("arbitrary",)),
        name="conv_module",
    )(u, u, u, dw_w, dw_b, ln_g, ln_b)


def _row_pick(idx, rows):
    out = jnp.zeros_like(rows[0])
    for e, r in enumerate(rows):
        out = jnp.where(idx == e, r, out)
    return out


def _argmax_rows(rows):
    best = rows[0]
    idx = jnp.zeros(rows[0].shape, jnp.int32)
    for e in range(1, len(rows)):
        upd = rows[e] > best
        idx = jnp.where(upd, e, idx)
        best = jnp.where(upd, rows[e], best)
    return idx


def _route(logits_t, bias):
    scores = _sigmoid(logits_t)
    sel = scores + bias
    s_rows = [scores[e:e + 1, :] for e in range(N_EXP)]
    rows = [sel[e:e + 1, :] for e in range(N_EXP)]
    grp = []
    for g in range(N_GROUPS):
        r = rows[g * EXP_PER_GROUP:(g + 1) * EXP_PER_GROUP]
        best = None
        for a in range(EXP_PER_GROUP):
            for b in range(a + 1, EXP_PER_GROUP):
                s = r[a] + r[b]
                best = s if best is None else jnp.maximum(best, s)
        grp.append(best)
    gidx = _argmax_rows(grp)
    masked = [jnp.where(gidx == e // EXP_PER_GROUP, rows[e], NEG) for e in range(N_EXP)]
    i1 = _argmax_rows(masked)
    i2 = _argmax_rows([jnp.where(i1 == e, -jnp.inf, masked[e]) for e in range(N_EXP)])
    w1 = _row_pick(i1, s_rows)
    w2 = _row_pick(i2, s_rows)
    tot = w1 + w2
    return i1, i2, w1 / tot, w2 / tot


def _merge_body(ac_ref, al_ref, c_ref, g_ref, xc_ref, xl_ref, mod_ref, n2_ref, wap_ref, wcp_ref, wo_ref, wr_ref,
                x1_ref, h_ref, lg_ref):
    step = pl.program_id(0)
    row = _mod_row(step, TM_MERGE)
    g1 = mod_ref[pl.ds(row, 1), 2 * D:3 * D]
    sh2 = mod_ref[pl.ds(row, 1), 3 * D:4 * D]
    sc2 = mod_ref[pl.ds(row, 1), 4 * D:5 * D]
    is_ctx = step < T_CTX // TM_MERGE
    a = jnp.where(is_ctx, ac_ref[...], al_ref[...])
    attn = jnp.dot(a, wap_ref[...], preferred_element_type=F32)
    conv = jnp.dot(c_ref[...], wcp_ref[...], preferred_element_type=F32)
    merged = g_ref[:, 0:D].astype(F32) * attn + g_ref[:, D:2 * D].astype(F32) * conv
    mix = jnp.dot(merged.astype(BF16), wo_ref[...], preferred_element_type=F32)
    x1 = jnp.where(is_ctx, xc_ref[...], xl_ref[...]) + g1 * mix
    x1_ref[...] = x1
    ms = jnp.mean(x1 * x1, axis=-1, keepdims=True)
    h = (x1 * lax.rsqrt(ms + EPS) * n2_ref[...]) * (1.0 + sc2) + sh2
    h_ref[...] = h.astype(BF16)
    lg_ref[...] = lax.dot_general(wr_ref[...], h, (((1,), (1,)), ((), ())),
                                  precision=lax.Precision.HIGHEST, preferred_element_type=F32)


def _merge(attn_ctx, attn_lat, conv_o, gates, x_ctx, x_lat, lat_off, mod_l, n2, wap, wcp, wo, wr_t):
    tm = TM_MERGE
    n_ctx = T_CTX // tm
    full = lambda shape: pl.BlockSpec(shape, lambda i: (0,) * len(shape))
    row = lambda w: pl.BlockSpec((tm, w), lambda i: (i, 0))
    a_ctx = pl.BlockSpec((tm, ATTN_W), lambda i: (jnp.minimum(i, n_ctx - 1), 0))
    a_lat = pl.BlockSpec((tm, ATTN_W), lambda i: (jnp.maximum(i - n_ctx, 0), 0))
    return pl.pallas_call(
        _merge_body,
        out_shape=(
            jax.ShapeDtypeStruct((T_ALL, D), F32),
            jax.ShapeDtypeStruct((T_ALL, D), BF16),
            jax.ShapeDtypeStruct((N_EXP, T_ALL), F32),
        ),
        grid=(T_ALL // tm,),
        in_specs=[a_ctx, a_lat, row(CONV_W), row(2 * D), *_stream_specs(tm, lat_off),
                  full((MOD_ROWS, 6 * D)), full((1, D)),
                  full((ATTN_W, D)), full((CONV_W, D)), full((D, D)), full((N_EXP, D))],
        out_specs=(row(D), row(D), pl.BlockSpec((N_EXP, tm), lambda i: (0, i))),
        compiler_params=pltpu.CompilerParams(dimension_semantics=("arbitrary",), vmem_limit_bytes=VMEM_LIMIT),
        name="merge",
    )(attn_ctx, attn_lat, conv_o, gates, x_ctx, x_lat, mod_l, n2, wap, wcp, wo, wr_t)


def _router_body(lg_ref, rb_ref, lp_ref, tok_ref, cnt_ref):
    tm = TM_ROUTE
    i1, i2, w1, w2 = _route(lg_ref[...], rb_ref[...])

    eio = lax.broadcasted_iota(jnp.int32, (N_EXP, tm), 0)
    oh1 = eio == i1
    oh2 = eio == i2
    oh = (oh1 | oh2).astype(F32)
    nc = ROUTE_CHUNK
    tri = (lax.broadcasted_iota(jnp.int32, (nc, nc), 0) < lax.broadcasted_iota(jnp.int32, (nc, nc), 1)).astype(BF16)
    lane = lax.broadcasted_iota(jnp.int32, (N_EXP, 128), 1)
    esub = lax.broadcasted_iota(jnp.int32, (N_EXP, 1), 0)
    counts = jnp.zeros((N_EXP, 128), F32)
    pos = []
    for s0 in range(0, tm, SORT_TL):
        base = jnp.zeros((N_EXP, 1), F32)
        ranks = []
        for c0 in range(s0, s0 + SORT_TL, nc):
            ohc = oh[:, c0:c0 + nc]
            ranks.append(base + jnp.dot(ohc.astype(BF16), tri, preferred_element_type=F32))
            base = base + jnp.sum(ohc, axis=1, keepdims=True)
        counts = jnp.where(lane == s0 // SORT_TL, base, counts)
        padded = jnp.floor((base + (CHUNK - 1)) * (1.0 / CHUNK)) * CHUNK
        run_start = jnp.zeros((N_EXP, 1), F32)
        for e in range(N_EXP - 1):
            run_start = run_start + jnp.where(esub > e, padded[e:e + 1, :], 0.0)
        pos.append(jnp.concatenate(ranks, axis=1) + run_start)
    pos = jnp.concatenate(pos, axis=1)
    p1 = jnp.sum(jnp.where(oh1, pos, 0.0), axis=0, keepdims=True)
    p2 = jnp.sum(jnp.where(oh2, pos, 0.0), axis=0, keepdims=True)
    cnt_ref[0] = counts

    sub = lax.broadcasted_iota(jnp.int32, (8, tm), 0)
    lp_ref[...] = jnp.where(sub == 0, p1.astype(jnp.int32), jnp.where(sub == 1, p2.astype(jnp.int32), 0))
    rows = jnp.where(sub == 0, p1, jnp.where(sub == 1, p2, jnp.where(sub == 2, w1, jnp.where(sub == 3, w2, 0.0))))
    tok_ref[...] = jnp.transpose(rows)


def _router(logits_t, rb):
    tm = TM_ROUTE
    lanes = lambda rows: pl.BlockSpec((rows, tm), lambda i: (0, i))
    return pl.pallas_call(
        _router_body,
        out_shape=(
            jax.ShapeDtypeStruct((8, T_ALL), jnp.int32),
            jax.ShapeDtypeStruct((T_ALL, 8), F32),
            jax.ShapeDtypeStruct((T_ALL // tm, N_EXP, 128), F32),
        ),
        grid=(T_ALL // tm,),
        in_specs=[lanes(N_EXP), pl.BlockSpec((N_EXP, 1), lambda i: (0, 0))],
        out_specs=(lanes(8), pl.BlockSpec((tm, 8), lambda i: (i, 0)),
                   pl.BlockSpec((1, N_EXP, 128), lambda i: (i, 0, 0))),
        compiler_params=pltpu.CompilerParams(dimension_semantics=("arbitrary",)),
        name="router",
    )(logits_t, rb)


def _sort_tables(cnt):
    per = TM_ROUTE // SORT_TL
    c = jnp.transpose(cnt[:, :, :per], (0, 2, 1)).reshape(N_SORT, N_EXP).astype(jnp.int32)
    pc = (c + (CHUNK - 1)) // CHUNK * CHUNK
    loff = jnp.cumsum(pc, axis=1) - pc
    used = jnp.sum(pc, axis=1)
    seg = jnp.sum(pc, axis=0)
    ends = jnp.cumsum(seg)
    starts = ends - seg
    gstart = starts[None, :] + jnp.cumsum(pc, axis=0) - pc
    j = jnp.arange(N_CHUNK, dtype=jnp.int32)[None, :, None]
    lo = (loff // CHUNK)[:, None, :]
    in_run = (j >= lo) & (j < lo + (pc // CHUNK)[:, None, :])
    dst = jnp.sum(jnp.where(in_run, (gstart // CHUNK)[:, None, :] + j - lo, 0), axis=-1)
    return (dst.reshape(-1).astype(jnp.int32), (used // CHUNK).astype(jnp.int32),
            starts.astype(jnp.int32), ends.astype(jnp.int32))


def _run_copies(dst_ref, used_ref, tile, make_copy, act):
    def body(j, carry):
        act(make_copy(j, dst_ref[tile * N_CHUNK + j]))
        return carry

    lax.fori_loop(0, used_ref[tile], body, 0)


def _start_copies(dst_ref, used_ref, tile, make_copy):
    n = used_ref[tile]

    def body(j, carry):
        make_copy(2 * j, dst_ref[tile * N_CHUNK + 2 * j]).start(priority=0)
        make_copy(2 * j + 1, dst_ref[tile * N_CHUNK + 2 * j + 1]).start(priority=1)
        return carry

    lax.fori_loop(0, lax.shift_right_logical(n, 1), body, 0)

    @pl.when((n & 1) == 1)
    def _():
        make_copy(n - 1, dst_ref[tile * N_CHUNK + n - 1]).start(priority=0)


def _sort_body(dst_ref, used_ref, tail_ref, lp_ref, h_ref, xs_hbm, xl_ref, sem):
    i = pl.program_id(0)
    slot = i % 2
    last = pl.num_programs(0) - 1
    r = lax.broadcasted_iota(jnp.int32, (CAP, SORT_TL), 0)
    place = ((r == lp_ref[0:1, :]) | (r == lp_ref[1:2, :])).astype(BF16)
    xl_ref[slot] = jnp.dot(place, h_ref[...], preferred_element_type=F32).astype(BF16)

    def copy_from(buf):
        def make(src, dst):
            return pltpu.make_async_copy(xl_ref.at[buf, pl.ds(src * CHUNK, CHUNK), :],
                                         xs_hbm.at[pl.ds(dst * CHUNK, CHUNK), :], sem.at[buf])
        return make

    def tail_copies(act):
        def body(j, carry):
            act(copy_from(slot)(N_CHUNK - 1, tail_ref[0] + j))
            return carry
        lax.fori_loop(0, tail_ref[1], body, 0)

    _start_copies(dst_ref, used_ref, i, copy_from(slot))

    @pl.when(i == last)
    def _():
        tail_copies(lambda c: c.start())

    @pl.when(i > 0)
    def _():
        _run_copies(dst_ref, used_ref,i - 1, copy_from(1 - slot), lambda c: c.wait())

    @pl.when(i == last)
    def _():
        _run_copies(dst_ref, used_ref,i, copy_from(slot), lambda c: c.wait())
        tail_copies(lambda c: c.wait())


def _sort_scatter(dst, used, tail, lpos8, h):
    return pl.pallas_call(
        _sort_body,
        out_shape=jax.ShapeDtypeStruct((XS_ROWS, D), BF16),
        grid_spec=pltpu.PrefetchScalarGridSpec(
            num_scalar_prefetch=3,
            grid=(N_SORT,),
            in_specs=[pl.BlockSpec((8, SORT_TL), lambda i, *_: (0, i)),
                      pl.BlockSpec((SORT_TL, D), lambda i, *_: (i, 0))],
            out_specs=pl.BlockSpec(memory_space=pl.ANY),
            scratch_shapes=[pltpu.VMEM((2, CAP, D), BF16), pltpu.SemaphoreType.DMA((2,))],
        ),
        compiler_params=pltpu.CompilerParams(dimension_semantics=("arbitrary",), vmem_limit_bytes=VMEM_LIMIT),
        name="sort_scatter",
    )(dst, used, tail, lpos8, h)


def _ffn_body(tile_ref, exp_ref, lo_ref, hi_ref, first_ref, mode_ref, fresh_ref,
              x_ref, wg_ref, wu_ref, wd_ref, o_ref, wg_b, wu_b, wd_b):
    u = pl.program_id(0)

    @pl.when(mode_ref[u] == 2)
    def _():
        o_ref[...] = jnp.zeros_like(o_ref)

    @pl.when(fresh_ref[u] == 1)
    def _():
        wg_b[...] = wg_ref[0, 0].astype(BF16)
        wu_b[...] = wu_ref[0, 0].astype(BF16)
        wd_b[...] = wd_ref[0, 0].astype(BF16)

    @pl.when(mode_ref[u] == 1)
    def _():
        parts = []
        for r0 in range(0, TM_FFN, FFN_SUB):
            x = x_ref[r0:r0 + FFN_SUB, :]
            g = jnp.dot(x, wg_b[...], preferred_element_type=F32)
            up = jnp.dot(x, wu_b[...], preferred_element_type=F32)
            rows = r0 + lax.broadcasted_iota(jnp.int32, (FFN_SUB, 1), 0)
            mine = (rows >= lo_ref[u]) & (rows < hi_ref[u])
            hid = jnp.where(mine, g * _sigmoid(g) * up, 0.0).astype(BF16)
            parts.append(jnp.dot(hid, wd_b[...], preferred_element_type=F32).astype(BF16))
        y = jnp.concatenate(parts, axis=0)

        @pl.when(first_ref[u] == 1)
        def _():
            o_ref[...] = y

        @pl.when(first_ref[u] == 0)
        def _():
            o_ref[...] += y


def _ffn(units, xs, wg, wu, wd, layer):
    xmap = lambda u, tile, *_: (tile[u], 0)
    wmap = lambda u, tile, exp, *_: (layer, exp[u], 0, 0)
    return pl.pallas_call(
        _ffn_body,
        out_shape=jax.ShapeDtypeStruct((XS_ROWS, D), BF16),
        grid_spec=pltpu.PrefetchScalarGridSpec(
            num_scalar_prefetch=7,
            grid=(N_UNITS,),
            in_specs=[pl.BlockSpec((TM_FFN, D), xmap),
                      pl.BlockSpec((1, 1, D, D_EXP), wmap), pl.BlockSpec((1, 1, D, D_EXP), wmap),
                      pl.BlockSpec((1, 1, D_EXP, D), wmap)],
            out_specs=pl.BlockSpec((TM_FFN, D), xmap),
            scratch_shapes=[pltpu.VMEM((D, D_EXP), BF16), pltpu.VMEM((D, D_EXP), BF16), pltpu.VMEM((D_EXP, D), BF16)],
        ),
        compiler_params=pltpu.CompilerParams(dimension_semantics=("arbitrary",), vmem_limit_bytes=VMEM_LIMIT),
        name="expert_ffn",
    )(*units, xs, wg, wu, wd)


def _ffn_units(starts, ends):
    total_rows = ends[-1]
    t0 = jnp.arange(N_FFN_TILES, dtype=jnp.int32) * TM_FFN
    t1 = jnp.minimum(t0 + TM_FFN, total_rows) - 1
    e_first = jnp.sum(ends[None, :] <= t0[:, None], axis=1).astype(jnp.int32)
    e_last = jnp.sum(ends[None, :] <= t1[:, None], axis=1).astype(jnp.int32)
    n_per = jnp.where(t0 < total_rows, e_last - e_first + 1, 0)
    u_end = jnp.cumsum(n_per)
    u_start = u_end - n_per
    total = u_end[-1]
    u = jnp.arange(N_UNITS, dtype=jnp.int32)
    uc = jnp.minimum(u, total - 1)
    tile = jnp.sum(u_end[None, :] <= uc[:, None], axis=1).astype(jnp.int32)
    exp = e_first[tile] + (uc - u_start[tile])
    lo = jnp.clip(starts[exp] - tile * TM_FFN, 0, TM_FFN)
    hi = jnp.clip(ends[exp] - tile * TM_FFN, 0, TM_FFN)
    first = ((uc == u_start[tile]) & (u < total)).astype(jnp.int32)
    exp = exp.astype(jnp.int32)
    fresh = jnp.concatenate([jnp.ones((1,), jnp.int32), (exp[1:] != exp[:-1]).astype(jnp.int32)])
    spare = jnp.sum(t0 < total_rows).astype(jnp.int32) + (u - total)
    mode = jnp.where(u < total, 1, jnp.where(spare < N_FFN_TILES, 2, 0)).astype(jnp.int32)
    tile = jnp.where(u < total, tile, jnp.minimum(spare, N_FFN_TILES - 1)).astype(jnp.int32)
    return tile, exp, lo.astype(jnp.int32), hi.astype(jnp.int32), first, mode, fresh


def _combine_body(dst_ref, used_ref, tok_ref, x1_ref, mod_ref, ys_hbm, o_ref, yl_ref, sem):
    i = pl.program_id(0)
    slot = i % 2
    last = pl.num_programs(0) - 1

    def copy_into(buf):
        def make(loc, glob):
            return pltpu.make_async_copy(ys_hbm.at[pl.ds(glob * CHUNK, CHUNK), :],
                                         yl_ref.at[buf, pl.ds(loc * CHUNK, CHUNK), :], sem.at[buf])
        return make

    @pl.when(i == 0)
    def _():
        _start_copies(dst_ref, used_ref, i, copy_into(slot))

    @pl.when(i < last)
    def _():
        _start_copies(dst_ref, used_ref, i + 1, copy_into(1 - slot))

    row = _mod_row(i, SORT_TL)
    g2 = mod_ref[pl.ds(row, 1), 5 * D:6 * D]
    lane = lax.broadcasted_iota(jnp.int32, (SORT_TL, CAP), 1)
    slot1 = tok_ref[:, 0:1].astype(jnp.int32)
    slot2 = tok_ref[:, 1:2].astype(jnp.int32)
    comb = (jnp.where(lane == slot1, tok_ref[:, 2:3], 0.0)
            + jnp.where(lane == slot2, tok_ref[:, 3:4], 0.0)).astype(BF16)

    _run_copies(dst_ref, used_ref,i, copy_into(slot), lambda c: c.wait())

    def clear(j, carry):
        yl_ref[slot, pl.ds(pl.multiple_of(j * CHUNK, CHUNK), CHUNK), :] = jnp.zeros((CHUNK, D), BF16)
        return carry

    lax.fori_loop(used_ref[i], N_CHUNK, clear, 0)
    y = jnp.dot(comb, yl_ref[slot], preferred_element_type=F32)
    o_ref[...] = x1_ref[...] + g2 * y


def _combine(dst, used, tok8, x1, mod_l, ys):
    tok = lambda w: pl.BlockSpec((SORT_TL, w), lambda i, *_: (i, 0))
    return pl.pallas_call(
        _combine_body,
        out_shape=jax.ShapeDtypeStruct((T_ALL, D), F32),
        grid_spec=pltpu.PrefetchScalarGridSpec(
            num_scalar_prefetch=2,
            grid=(N_SORT,),
            in_specs=[tok(8), tok(D),
                      pl.BlockSpec((MOD_ROWS, 6 * D), lambda i, *_: (0, 0)),
                      pl.BlockSpec(memory_space=pl.ANY)],
            out_specs=tok(D),
            scratch_shapes=[pltpu.VMEM((2, CAP, D), BF16), pltpu.SemaphoreType.DMA((2,))],
        ),
        compiler_params=pltpu.CompilerParams(dimension_semantics=("arbitrary",), vmem_limit_bytes=VMEM_LIMIT),
        name="combine",
    )(dst, used, tok8, x1, mod_l, ys)


def _final_norm_body(x_ref, g_ref, o_ref):
    x = x_ref[...]
    ms = jnp.mean(x * x, axis=-1, keepdims=True)
    o_ref[...] = x * lax.rsqrt(ms + EPS) * g_ref[...]


def _final_norm(x, gain, row0, rows):
    tm = 512
    first = row0 // tm
    return pl.pallas_call(
        _final_norm_body,
        out_shape=jax.ShapeDtypeStruct((rows, D), F32),
        grid=(rows // tm,),
        in_specs=[pl.BlockSpec((tm, D), lambda i: (first + i, 0)), pl.BlockSpec((1, D), lambda i: (0, 0))],
        out_specs=pl.BlockSpec((tm, D), lambda i: (i, 0)),
        name="final_norm",
    )(x, gain)


def _rope_tables():
    pos = jnp.arange(LAT_L)
    rowp = (pos // GRID_W).astype(F32)
    colp = (pos % GRID_W).astype(F32)
    pairs = HD // 4
    inv = ROPE_BASE ** (-jnp.arange(pairs, dtype=F32) / pairs)
    ang = jnp.concatenate([rowp[:, None] * inv] * 2 + [colp[:, None] * inv] * 2, axis=-1)
    cos = jnp.tile(jnp.cos(ang), (1, N_HEADS))
    sin = jnp.tile(jnp.sin(ang), (1, N_HEADS))
    cos = jnp.concatenate([jnp.ones((TM_IN, ATTN_W), F32), cos], axis=0)
    sin = jnp.concatenate([jnp.zeros((TM_IN, ATTN_W), F32), sin], axis=0)
    return cos, sin


def _head_matrices():
    i = jnp.arange(ATTN_W)
    bd = (i[:, None] // HD == i[None, :] // HD).astype(BF16)
    half = HD // 4
    j = i[None, :]
    src = i[:, None]
    first = (j % (2 * half)) < half
    rm = jnp.where(first & (src == j + half), -1.0, 0.0) + jnp.where(~first & (src == j - half), 1.0, 0.0)
    return bd, rm.astype(BF16)


def kernel(x_prompt, x_sample, cache_k, cache_v, c, c_ctx, w_ada, b_ada, norm1, norm2, w_in, q_norm, k_norm,
           sink, w_attn_proj, dw_w, dw_b, cln_g, cln_b, w_conv_proj, w_out, w_router, router_bias,
           w_e_gate, w_e_up, w_e_down, final_norm):
    x_ctx, x_lat = x_prompt.reshape(T_CTX, D), x_sample.reshape(T_LAT, D)
    cond = jnp.concatenate([c_ctx[None, :], c, jnp.zeros((MOD_ROWS - 1 - LAT_B, D), F32)], axis=0)
    mod = _ada_table(cond, w_ada, b_ada)
    cos_t, sin_t = _rope_tables()
    bd, rm = _head_matrices()
    w_in_b = w_in.astype(BF16)
    wap_b = w_attn_proj.astype(BF16)
    wcp_b = w_conv_proj.astype(BF16)
    wo_b = w_out.astype(BF16)
    wr_t = w_router.T
    rb = router_bias.reshape(N_EXP, 1)
    ck = cache_k.reshape(LAT_B, DEPTH, PAST, KV_W)
    cv = cache_v.reshape(LAT_B, DEPTH, PAST, KV_W)

    new_k, new_v = [], []
    for l in range(DEPTH):
        merged = l > 0
        q, k, v, u, gates = _inproj(x_ctx, x_lat, T_CTX // TM_IN if merged else 0, mod[l], norm1[l][None, :], w_in_b[l],
                                    jnp.tile(q_norm[l], N_HEADS)[None, :], jnp.tile(k_norm[l], N_KV)[None, :],
                                    cos_t, sin_t, bd, rm)
        new_k.append(k[:T_CTX].reshape(CTX_B, CTX_L, N_KV, HD))
        new_v.append(v[:T_CTX].reshape(CTX_B, CTX_L, N_KV, HD))
        o_ctx = _attn_ctx(sink[l], q, k, v)
        o_lat = _attn_lat(sink[l], q, k, v, ck, cv, l)
        conv_o = _conv(u, dw_w[l], dw_b[l][None, :], cln_g[l][None, :], cln_b[l][None, :])
        x1, h, logits_t = _merge(o_ctx, o_lat, conv_o, gates, x_ctx, x_lat, T_CTX // TM_MERGE if merged else 0,
                                 mod[l], norm2[l][None, :], wap_b[l], wcp_b[l], wo_b[l], wr_t)
        lpos, tok, cnt = _router(logits_t, rb)
        dst, used, starts, ends = _sort_tables(cnt)
        rows_ch = ends[-1] // CHUNK
        tail = jnp.stack([rows_ch, XS_ROWS // CHUNK - rows_ch]).astype(jnp.int32)
        xs = _sort_scatter(dst, used, tail, lpos, h)
        ys = _ffn(_ffn_units(starts, ends), xs, w_e_gate, w_e_up, w_e_down, l)
        x_ctx = x_lat = _combine(dst, used, tok, x1, mod[l], ys)

    gain = final_norm[None, :]
    y_prompt = _final_norm(x_ctx, gain, 0, T_CTX).reshape(CTX_B, CTX_L, D)
    y_sample = _final_norm(x_lat, gain, T_CTX, T_LAT).reshape(LAT_B, LAT_L, D)
    return y_prompt, y_sample, jnp.stack(new_k, axis=1), jnp.stack(new_v, axis=1)
```
